```python
import math
import jax, jax.numpy as jnp
from jax import lax
import numpy as np

D_MODEL = 1024
BATCH = 8
SEQ = 4096
DEPTH = 2

HEAD_DIM = 64
RWKV_HEADS = 6
RWKV_WIDTH = RWKV_HEADS * HEAD_DIM
SB_HEADS = 6
SB_WIDTH = SB_HEADS * HEAD_DIM
POOL_WINDOWS = (2, 4, 8, 16)
POOL_GROUPS = len(POOL_WINDOWS)
POOL_WIDTH = D_MODEL - RWKV_WIDTH - SB_WIDTH
POOL_GROUP_DIM = POOL_WIDTH // POOL_GROUPS
DECAY_LORA = 64
AAA_LORA = 64
GATE_LORA = 128
MV_LORA = 32
RWKV_COLS = 3 * RWKV_WIDTH + DECAY_LORA + AAA_LORA + GATE_LORA
SB_COLS = 3 * SB_WIDTH
IN_COLS = RWKV_COLS + SB_COLS + POOL_WIDTH
D_FF = 2816
CONV_WIDTH = 3
Q_BLOCK = 128
NORM_EPS = 1e-6
LN_X_EPS = 64e-5
L2_EPS = 1e-12

kernel_name = "hymba_rwkv7_stickbreak_pool_convffn"

F32 = jnp.float32


def rms_norm(x, g, eps=NORM_EPS):
    xf = x.astype(F32)
    y = xf * lax.rsqrt(jnp.mean(xf * xf, axis=-1, keepdims=True) + eps) * g.astype(F32)
    return y.astype(x.dtype)


def token_shift(p):
    return jnp.pad(p, ((0, 0), (1, 0), (0, 0)))[:, :-1]


def wkv7_scan(r, decay, k, v, a_vec, b_vec):
    def step(S, inp):
        r_t, w_t, k_t, v_t, a_t, b_t = inp
        Sa = jnp.einsum('bhij,bhj->bhi', S, a_t)
        S = (S * w_t[:, :, None, :] + Sa[..., :, None] * b_t[..., None, :]
             + v_t[..., :, None] * k_t[..., None, :])
        y = jnp.einsum('bhij,bhj->bhi', S, r_t)
        return S, y
    B, T, H, N = r.shape
    xs = tuple(jnp.moveaxis(t, 1, 0) for t in (r, decay, k, v, a_vec, b_vec))
    S0 = jnp.zeros((B, H, N, N), F32)
    _, ys = lax.scan(step, S0, xs)
    return jnp.moveaxis(ys, 0, 1)


def rwkv7_time_mix(p, mu, w0, w2, a0, a2, g2, k_k, k_a, r_k, lnx_w, lnx_b, v_first, vmix):
    B, T, _ = p.shape
    H, N = RWKV_HEADS, HEAD_DIM
    p = p + (token_shift(p) - p) * mu
    cuts = np.cumsum([RWKV_WIDTH, RWKV_WIDTH, RWKV_WIDTH, DECAY_LORA, AAA_LORA])
    r, k, v, xw, xa, xg = jnp.split(p, cuts, axis=-1)
    w = -jax.nn.softplus(-(w0 + jnp.tanh(xw) @ w2)) - 0.5
    decay = jnp.exp(-jnp.exp(w))
    a = jax.nn.sigmoid(a0 + xa @ a2)
    g = jax.nn.sigmoid(xg) @ g2
    if vmix is None:
        v_first = v
    else:
        v0, v1, v2 = vmix
        v = v + (v_first - v) * jax.nn.sigmoid(v0 + (v @ v1) @ v2)
    kk = (k * k_k).reshape(B, T, H, N)
    kk = kk / jnp.maximum(jnp.sqrt(jnp.sum(kk * kk, axis=-1, keepdims=True)), L2_EPS)
    k = k * (1.0 + (a - 1.0) * k_a)
    rh, kh, vh = (t.reshape(B, T, H, N) for t in (r, k, v))
    ah = a.reshape(B, T, H, N)
    y = wkv7_scan(rh, decay.reshape(B, T, H, N), kh, vh, -kk, kk * ah)
    mean = jnp.mean(y, axis=-1, keepdims=True)
    var = jnp.mean(jnp.square(y - mean), axis=-1, keepdims=True)
    y = ((y - mean) * lax.rsqrt(var + LN_X_EPS)).reshape(B, T, RWKV_WIDTH) * lnx_w + lnx_b
    bonus = jnp.sum(rh * kh * r_k, axis=-1, keepdims=True) * vh
    y = (y + bonus.reshape(B, T, RWKV_WIDTH)) * g
    return y, v_first


def stick_breaking_attention(q, k, v, q_gain, k_gain):
    q = rms_norm(q, q_gain)
    k = rms_norm(k, k_gain)
    q, k, v = (jnp.transpose(t, (0, 2, 1, 3)) for t in (q, k, v))
    T = q.shape[2]
    scale = HEAD_DIM ** -0.5
    outs = []
    for t0 in range(0, T, Q_BLOCK):
        lk = t0 + Q_BLOCK
        qb, kp, vp = q[:, :, t0:lk], k[:, :, :lk], v[:, :, :lk]
        z = jnp.einsum('bhqd,bhkd->bhqk', qb, kp) * scale
        causal = jnp.arange(lk)[None, :] < (t0 + jnp.arange(Q_BLOCK))[:, None]
        log_not = jnp.where(causal, jax.nn.log_sigmoid(-z), 0.0)
        log_remaining = lax.cumsum(log_not, axis=3, reverse=True) - log_not
        A = jnp.where(causal, jnp.exp(jax.nn.log_sigmoid(z) + log_remaining), 0.0)
        outs.append(jnp.einsum('bhqk,bhkd->bhqd', A, vp))
    o = jnp.concatenate(outs, axis=2)
    B, H, _, N = o.shape
    return jnp.transpose(o, (0, 2, 1, 3)).reshape(B, T, H * N)


def multiscale_pool(u, lin_w, lin_b, scale):
    B, T, _ = u.shape
    ug = u.reshape(B, T, POOL_GROUPS, POOL_GROUP_DIM)
    cs = jnp.pad(jnp.cumsum(ug, axis=1), ((0, 0), (1, 0), (0, 0), (0, 0)))
    pos = jnp.arange(T)
    pooled = []
    for g, w in enumerate(POOL_WINDOWS):
        start = jnp.maximum(pos + 1 - w, 0)
        win_sum = cs[:, 1:, g] - cs[:, start, g]
        count = jnp.minimum(pos + 1, w).astype(F32)[:, None]
        pooled.append(win_sum / count)
    pooled = jnp.stack(pooled, axis=2) - ug
    y = jnp.einsum('btgc,gcd->btgd', pooled, lin_w) + lin_b
    return y.reshape(B, T, POOL_WIDTH) * scale


def conv_glu_ffn(h, w_up, conv_w, conv_b, w_down):
    T = h.shape[1]
    up = h @ w_up
    padded = jnp.pad(up, ((0, 0), (CONV_WIDTH - 1, 0), (0, 0)))
    c = conv_b
    for i in range(CONV_WIDTH):
        c = c + padded[:, i:i + T] * conv_w[i]
    gate, val = jnp.split(c, 2, axis=-1)
    return (jax.nn.silu(gate) * val) @ w_down


def setup_inputs(seed: int = 0) -> dict:
    key = jax.random.key(seed)
    ks = jax.random.split(key, 32)
    nrm = lambda k, shape, s: jax.random.normal(k, shape, F32) * s
    L, Lv = DEPTH, DEPTH - 1
    return {
        "x": jax.random.normal(ks[0], (BATCH, SEQ, D_MODEL), F32),
        "ln1_g": 1.0 + nrm(ks[1], (L, D_MODEL), 0.05),
        "w_in": nrm(ks[2], (L, D_MODEL, IN_COLS), D_MODEL ** -0.5),
        "mu_shift": jax.random.uniform(ks[3], (L, RWKV_COLS), F32),
        "w0": jax.random.uniform(ks[4], (L, RWKV_WIDTH), F32, -6.0, 0.0),
        "w2": nrm(ks[5], (L, DECAY_LORA, RWKV_WIDTH), 0.1 * DECAY_LORA ** -0.5),
        "a0": nrm(ks[6], (L, RWKV_WIDTH), 0.1),
        "a2": nrm(ks[7], (L, AAA_LORA, RWKV_WIDTH), AAA_LORA ** -0.5),
        "g2": nrm(ks[8], (L, GATE_LORA, RWKV_WIDTH), GATE_LORA ** -0.5),
        "k_k": 0.85 + nrm(ks[9], (L, RWKV_WIDTH), 0.05),
        "k_a": 1.0 + nrm(ks[10], (L, RWKV_WIDTH), 0.05),
        "r_k": nrm(ks[11], (L, RWKV_HEADS, HEAD_DIM), 0.1),
        "lnx_w": 1.0 + nrm(ks[12], (L, RWKV_WIDTH), 0.05),
        "lnx_b": nrm(ks[13], (L, RWKV_WIDTH), 0.02),
        "v0": 1.0 + nrm(ks[14], (Lv, RWKV_WIDTH), 0.1),
        "v1": nrm(ks[15], (Lv, RWKV_WIDTH, MV_LORA), RWKV_WIDTH ** -0.5),
        "v2": nrm(ks[16], (Lv, MV_LORA, RWKV_WIDTH), MV_LORA ** -0.5),
        "q_gain": 1.0 + nrm(ks[17], (L, HEAD_DIM), 0.05),
        "k_gain": 1.0 + nrm(ks[18], (L, HEAD_DIM), 0.05),
        "pool_w": nrm(ks[19], (L, POOL_GROUPS, POOL_GROUP_DIM, POOL_GROUP_DIM), POOL_GROUP_DIM ** -0.5),
        "pool_b": nrm(ks[20], (L, POOL_GROUPS, POOL_GROUP_DIM), 0.02),
        "pool_scale": 1.0 + nrm(ks[21], (L, POOL_WIDTH), 0.1),
        "w_out": nrm(ks[22], (L, D_MODEL, D_MODEL), D_MODEL ** -0.5),
        "ln2_g": 1.0 + nrm(ks[23], (L, D_MODEL), 0.05),
        "w_up": nrm(ks[24], (L, D_MODEL, 2 * D_FF), D_MODEL ** -0.5),
        "conv_w": nrm(ks[25], (L, CONV_WIDTH, 2 * D_FF), CONV_WIDTH ** -0.5),
        "conv_b": nrm(ks[26], (L, 2 * D_FF), 0.02),
        "w_down": nrm(ks[27], (L, D_FF, D_MODEL), D_FF ** -0.5),
    }


def reference(x, ln1_g, w_in, mu_shift, w0, w2, a0, a2, g2, k_k, k_a, r_k, lnx_w, lnx_b,
              v0, v1, v2, q_gain, k_gain, pool_w, pool_b, pool_scale, w_out, ln2_g,
              w_up, conv_w, conv_b, w_down):
    B, T, _ = x.shape
    v_first = None
    for l in range(DEPTH):
        h = rms_norm(x, ln1_g[l])
        proj = (h @ w_in[l]).astype(F32)
        p_rwkv, p_sb, p_pool = jnp.split(proj, [RWKV_COLS, RWKV_COLS + SB_COLS], axis=-1)
        vmix = None if l == 0 else (v0[l - 1], v1[l - 1], v2[l - 1])
        y_rwkv, v_first = rwkv7_time_mix(p_rwkv, mu_shift[l], w0[l], w2[l], a0[l], a2[l], g2[l],
                                         k_k[l], k_a[l], r_k[l], lnx_w[l], lnx_b[l], v_first, vmix)
        q, k, v = (t.reshape(B, T, SB_HEADS, HEAD_DIM) for t in jnp.split(p_sb, 3, axis=-1))
        y_sb = stick_breaking_attention(q, k, v, q_gain[l], k_gain[l])
        y_pool = multiscale_pool(p_pool, pool_w[l], pool_b[l], pool_scale[l])
        mix = jnp.concatenate([y_rwkv, y_sb, y_pool], axis=-1).astype(x.dtype)
        x = x + mix @ w_out[l]
        x = x + conv_glu_ffn(rms_norm(x, ln2_g[l]), w_up[l], conv_w[l], conv_b[l], w_down[l])
    return x
```

```python
import functools

import jax
import jax.numpy as jnp
from jax import lax
from jax.experimental import pallas as pl
from jax.experimental.pallas import tpu as pltpu

F32 = jnp.float32
BF16 = jnp.bfloat16

D_MODEL = 1024
HEAD_DIM = 64
RWKV_HEADS = 6
RWKV_WIDTH = RWKV_HEADS * HEAD_DIM
SB_HEADS = 6
SB_WIDTH = SB_HEADS * HEAD_DIM
POOL_WINDOWS = (2, 4, 8, 16)
POOL_WIDTH = D_MODEL - RWKV_WIDTH - SB_WIDTH
POOL_GROUP_DIM = POOL_WIDTH // len(POOL_WINDOWS)
DECAY_LORA = 64
AAA_LORA = 64
GATE_LORA = 128
MV_LORA = 32
RWKV_COLS = 3 * RWKV_WIDTH + DECAY_LORA + AAA_LORA + GATE_LORA
SB_COLS = 3 * SB_WIDTH
IN_COLS = RWKV_COLS + SB_COLS + POOL_WIDTH
D_FF = 2816
NORM_EPS = 1e-6
LN_X_EPS = 64e-5
L2_EPS = 1e-12

SUBLANES = 8
LANES = 128
PAIR = 2 * HEAD_DIM
RWKV_PAIRS = RWKV_WIDTH // PAIR
SB_PAIRS = SB_WIDTH // PAIR
CHUNK = 64
POOL_HALO = 16
VMEM_LIMIT = 48 * 1024 * 1024

NN = (((1,), (0,)), ((), ()))
NT = (((1,), (1,)), ((), ()))
TN = (((0,), (0,)), ((), ()))


def _cparams(*sem):
    return pltpu.CompilerParams(dimension_semantics=sem, vmem_limit_bytes=VMEM_LIMIT)


def _dot(a, b, dims=NN):
    return lax.dot_general(a, b, dims, preferred_element_type=F32)


def _split2(x):
    hi = x.astype(BF16)
    lo = (x - hi.astype(F32)).astype(BF16)
    return hi, lo


def _split3(x):
    hi = x.astype(BF16)
    r1 = x - hi.astype(F32)
    mid = r1.astype(BF16)
    lo = (r1 - mid.astype(F32)).astype(BF16)
    return hi, mid, lo


def _dot_hp(a, b, dims=NN):
    ah, al = _split2(a)
    bh, bl = _split2(b)
    return _dot(ah, bh, dims) + (_dot(al, bh, dims) + _dot(ah, bl, dims))


def _dot_exact_rhs(a, b_bf16, dims=NN):
    h, m, l = _split3(a)
    return _dot(h, b_bf16, dims) + (_dot(m, b_bf16, dims) + _dot(l, b_bf16, dims))


def _dot_exact_lhs(a_bf16, b, dims=NN):
    h, m, l = _split3(b)
    return _dot(a_bf16, h, dims) + (_dot(a_bf16, m, dims) + _dot(a_bf16, l, dims))


def _sigmoid(x):
    return 1.0 / (1.0 + jnp.exp(-x))


def _softplus(x):
    return jnp.maximum(x, 0.0) + jnp.log(1.0 + jnp.exp(-jnp.abs(x)))


def _rms_rows(x, g):
    return x * lax.rsqrt(jnp.mean(x * x, axis=-1, keepdims=True) + NORM_EPS) * g


def _inproj_kernel(x_ref, g_ref, w_ref, o_ref):
    h = _rms_rows(x_ref[...], g_ref[...])
    o_ref[...] = _dot(h.astype(BF16), w_ref[...])


def _inproj(x2, g, w_bf16, tm, tn):
    n = x2.shape[0]
    return pl.pallas_call(
        _inproj_kernel,
        grid=(n // tm, IN_COLS // tn),
        in_specs=[pl.BlockSpec((tm, D_MODEL), lambda i, j: (i, 0)),
                  pl.BlockSpec((1, D_MODEL), lambda i, j: (0, 0)),
                  pl.BlockSpec((D_MODEL, tn), lambda i, j: (0, j))],
        out_specs=pl.BlockSpec((tm, tn), lambda i, j: (i, j)),
        out_shape=jax.ShapeDtypeStruct((n, IN_COLS), F32),
        compiler_params=_cparams("parallel", "arbitrary"),
        name="inproj",
    )(x2, g, w_bf16)


def _rwkv_prep_kernel(blocks_per_seq, has_vmix, *refs):
    if has_vmix:
        (p_ref, halo_ref, mu_ref, w0_ref, w2_ref, a0_ref, a2_ref, g2_ref, kk_ref, ka_ref, rk_ref,
         hsum_ref, vf_ref, v0_ref, v1_ref, v2_ref,
         r_o, lw_o, k_o, v_o, kk_o, a_o, g_o, bonus_o) = refs
    else:
        (p_ref, halo_ref, mu_ref, w0_ref, w2_ref, a0_ref, a2_ref, g2_ref, kk_ref, ka_ref, rk_ref,
         hsum_ref,
         r_o, lw_o, k_o, v_o, kk_o, a_o, g_o, bonus_o) = refs
    p = p_ref[...]
    tm = p.shape[0]
    first = (pl.program_id(0) % blocks_per_seq) == 0
    prev_last = jnp.where(first, 0.0, halo_ref[SUBLANES - 1:SUBLANES, :])
    row = lax.broadcasted_iota(jnp.int32, (tm, 1), 0)
    shifted = jnp.where(row == 0, prev_last, pltpu.roll(p, 1, axis=0))
    p = p + (shifted - p) * mu_ref[...]

    c0, c1, c2 = RWKV_WIDTH, 2 * RWKV_WIDTH, 3 * RWKV_WIDTH
    r = p[:, 0:c0]
    k = p[:, c0:c1]
    v = p[:, c1:c2]
    xwa = p[:, c2:c2 + DECAY_LORA + AAA_LORA]
    xw = xwa[:, :DECAY_LORA]
    xa = xwa[:, DECAY_LORA:]
    xg = p[:, c2 + DECAY_LORA + AAA_LORA:]

    w = -_softplus(-(w0_ref[...] + _dot_hp(jnp.tanh(xw), w2_ref[...]))) - 0.5
    lw_o[...] = -jnp.exp(w)
    a = _sigmoid(a0_ref[...] + _dot_hp(xa, a2_ref[...]))
    g_o[...] = _dot_hp(_sigmoid(xg), g2_ref[...])
    if has_vmix:
        mix = _sigmoid(v0_ref[...] + _dot_hp(_dot_hp(v, v1_ref[...]), v2_ref[...]))
        v = v + (vf_ref[...] - v) * mix
    hsum = hsum_ref[...]
    kk = k * kk_ref[...]
    kk = kk / jnp.maximum(jnp.sqrt(_dot_exact_rhs(kk * kk, hsum)), L2_EPS)
    k = k * (1.0 + (a - 1.0) * ka_ref[...])
    r_o[...] = r
    k_o[...] = k
    v_o[...] = v
    kk_o[...] = kk
    a_o[...] = a
    bonus_o[...] = _dot_exact_rhs(r * k * rk_ref[...], hsum) * v


def _rwkv_prep(proj, seq_len, tm, params, vmix):
    n = proj.shape[0]
    has_vmix = vmix is not None
    row = lambda i: (i, 0)
    const = lambda i: (0, 0)
    halo_idx = lambda i: (jnp.maximum(i * (tm // SUBLANES) - 1, 0), 0)
    vec = pl.BlockSpec((1, RWKV_WIDTH), const)
    in_specs = [pl.BlockSpec((tm, RWKV_COLS), row),
                pl.BlockSpec((SUBLANES, RWKV_COLS), halo_idx),
                pl.BlockSpec((1, RWKV_COLS), const),
                vec, pl.BlockSpec((DECAY_LORA, RWKV_WIDTH), const),
                vec, pl.BlockSpec((AAA_LORA, RWKV_WIDTH), const),
                pl.BlockSpec((GATE_LORA, RWKV_WIDTH), const),
                vec, vec, vec,
                pl.BlockSpec((RWKV_WIDTH, RWKV_WIDTH), const)]
    args = [proj, proj] + list(params)
    if has_vmix:
        in_specs += [pl.BlockSpec((tm, RWKV_WIDTH), row), vec,
                     pl.BlockSpec((RWKV_WIDTH, MV_LORA), const),
                     pl.BlockSpec((MV_LORA, RWKV_WIDTH), const)]
        args += list(vmix)
    out = jax.ShapeDtypeStruct((n, RWKV_WIDTH), F32)
    return pl.pallas_call(
        functools.partial(_rwkv_prep_kernel, seq_len // tm, has_vmix),
        grid=(n // tm,),
        in_specs=in_specs,
        out_specs=[pl.BlockSpec((tm, RWKV_WIDTH), row)] * 8,
        out_shape=[out] * 8,
        compiler_params=_cparams("parallel"),
        name="rwkv_prep",
    )(*args)


def _pair_rows(x, lane_lo):
    return jnp.concatenate([jnp.where(lane_lo, x, 0.0), jnp.where(lane_lo, 0.0, x)], axis=0)


def _rwkv_chunk_kernel(chunks, r_ref, lw_ref, k_ref, v_ref, kk_ref, a_ref, tri_ref,
                       q1_o, q2_o, g_o, e_o):
    two = 2 * CHUNK
    ri = lax.broadcasted_iota(jnp.int32, (two, two), 0)
    ci = lax.broadcasted_iota(jnp.int32, (two, two), 1)
    same = (ri // CHUNK) == (ci // CHUNK)
    strict = same & ((ci % CHUNK) < (ri % CHUNK))
    incl = same & ((ci % CHUNK) <= (ri % CHUNK))
    eye = (ri == ci).astype(F32)
    lane_lo = lax.broadcasted_iota(jnp.int32, (CHUNK, PAIR), 1) < HEAD_DIM
    tri = tri_ref[...]

    for c in range(chunks):
        rows = pl.ds(c * CHUNK, CHUNK)
        lw = lw_ref[rows, :]
        cl = _dot_exact_lhs(tri, lw)
        e_pos = jnp.exp(cl)
        e_prev = jnp.exp(cl - lw)
        e_neg = jnp.exp(-cl)
        kk = kk_ref[rows, :]
        at = -kk * e_prev
        bt = kk * a_ref[rows, :] * e_neg
        rt = r_ref[rows, :] * e_pos
        kt = k_ref[rows, :] * e_neg
        v = v_ref[rows, :]
        wc = e_pos[CHUNK - 1:CHUNK, :]

        for p in range(RWKV_PAIRS):
            ls = slice(p * PAIR, (p + 1) * PAIR)
            xa = _pair_rows(at[:, ls], lane_lo)
            xr = _pair_rows(rt[:, ls], lane_lo)
            vv = _pair_rows(v[:, ls], lane_lo)
            btp = bt[:, ls]
            ktp = kt[:, ls]
            yb = jnp.concatenate([btp, btp], axis=0)
            yk = jnp.concatenate([ktp, ktp], axis=0)
            a_ab = jnp.where(strict, _dot_hp(xa, yb, NT), 0.0)
            a_ak = jnp.where(strict, _dot_hp(xa, yk, NT), 0.0)
            m_rb = jnp.where(incl, _dot_hp(xr, yb, NT), 0.0)
            m_rk = jnp.where(incl, _dot_hp(xr, yk, NT), 0.0)

            t = eye + a_ab
            pw = a_ab
            for _ in range(CHUNK.bit_length() - 2):
                pw = _dot_hp(pw, pw)
                t = t + _dot_hp(t, pw)

            p1m = _dot_hp(t, xa)
            p2m = _dot_hp(t, _dot_hp(a_ak, vv))
            q1m = xr + _dot_hp(m_rb, p1m)
            q2m = _dot_hp(m_rb, p2m) + _dot_hp(m_rk, vv)
            q1_o[rows, ls] = q1m[:CHUNK] + q1m[CHUNK:]
            q2_o[rows, ls] = q2m[:CHUNK] + q2m[CHUNK:]
            p1 = p1m[:CHUNK] + p1m[CHUNK:]
            p2 = p2m[:CHUNK] + p2m[CHUNK:]
            lt = jnp.concatenate([jnp.concatenate([p1, p2], axis=1),
                                  jnp.concatenate([jnp.zeros_like(p1), v[:, ls]], axis=1)], axis=0)
            rt2 = jnp.concatenate([btp, ktp], axis=0)
            ge = _dot_hp(lt, rt2, TN)
            wcp = wc[:, ls]
            g_o[c, p] = jnp.where(same, (eye + ge[:PAIR]) * wcp, 0.0)
            e_o[c, p] = jnp.where(same, ge[PAIR:] * wcp, 0.0)


def _rwkv_chunk(r, lw, k, v, kk, a, tri, chunks):
    n = r.shape[0]
    tc = chunks * CHUNK
    row = lambda i: (i, 0)
    tok = pl.BlockSpec((tc, RWKV_WIDTH), row)
    mat = pl.BlockSpec((chunks, RWKV_PAIRS, PAIR, PAIR), lambda i: (i, 0, 0, 0))
    tok_shape = jax.ShapeDtypeStruct((n, RWKV_WIDTH), F32)
    mat_shape = jax.ShapeDtypeStruct((n // CHUNK, RWKV_PAIRS, PAIR, PAIR), F32)
    return pl.pallas_call(
        functools.partial(_rwkv_chunk_kernel, chunks),
        grid=(n // tc,),
        in_specs=[tok] * 6 + [pl.BlockSpec((CHUNK, CHUNK), lambda i: (0, 0))],
        out_specs=[tok, tok, mat, mat],
        out_shape=[tok_shape, tok_shape, mat_shape, mat_shape],
        compiler_params=_cparams("parallel"),
        name="rwkv_chunk",
    )(r, lw, k, v, kk, a, tri)


def _rwkv_scan_kernel(chunks, q1_ref, q2_ref, g_ref, e_ref, bonus_ref, gate_ref, hmean_ref,
                      lnw_ref, lnb_ref, o_ref, s_ref, y_ref):
    @pl.when(pl.program_id(1) == 0)
    def _():
        s_ref[...] = jnp.zeros_like(s_ref)

    for c in range(chunks):
        rows = pl.ds(c * CHUNK, CHUNK)
        for p in range(RWKV_PAIRS):
            ls = slice(p * PAIR, (p + 1) * PAIR)
            s = s_ref[p]
            y_ref[rows, ls] = _dot_hp(q1_ref[rows, ls], s, NT) + q2_ref[rows, ls]
            s_ref[p] = _dot_hp(s, g_ref[c, p]) + e_ref[c, p]

    y = y_ref[...]
    hmean = hmean_ref[...]
    mean = _dot_exact_rhs(y, hmean)
    yc = y - mean
    var = _dot_exact_rhs(yc * yc, hmean)
    yn = yc * lax.rsqrt(var + LN_X_EPS) * lnw_ref[...] + lnb_ref[...]
    o_ref[...] = (yn + bonus_ref[...]) * gate_ref[...]


def _rwkv_scan(q1, q2, g, e, bonus, gate, hmean, lnw, lnb, batch, seq_len, chunks):
    n = q1.shape[0]
    tc = chunks * CHUNK
    steps = seq_len // tc
    row = lambda b, i: (b * steps + i, 0)
    const = lambda b, i: (0, 0)
    tok = pl.BlockSpec((tc, RWKV_WIDTH), row)
    mat = pl.BlockSpec((chunks, RWKV_PAIRS, PAIR, PAIR), lambda b, i: (b * steps + i, 0, 0, 0))
    vec = pl.BlockSpec((1, RWKV_WIDTH), const)
    return pl.pallas_call(
        functools.partial(_rwkv_scan_kernel, chunks),
        grid=(batch, steps),
        in_specs=[tok, tok, mat, mat, tok, tok,
                  pl.BlockSpec((RWKV_WIDTH, RWKV_WIDTH), const), vec, vec],
        out_specs=tok,
        out_shape=jax.ShapeDtypeStruct((n, RWKV_WIDTH), F32),
        scratch_shapes=[pltpu.VMEM((RWKV_PAIRS, PAIR, PAIR), F32),
                        pltpu.VMEM((tc, RWKV_WIDTH), F32)],
        compiler_params=_cparams("parallel", "arbitrary"),
        name="rwkv_scan",
    )(q1, q2, g, e, bonus, gate, hmean, lnw, lnb)


def _sb_prep_kernel(q_ref, k_ref, qg_ref, kg_ref, hmean_ref, qo_ref, ko_ref):
    hmean = hmean_ref[...]
    q = q_ref[...]
    k = k_ref[...]
    qn = q * lax.rsqrt(_dot_exact_rhs(q * q, hmean) + NORM_EPS) * qg_ref[...]
    qo_ref[...] = qn * (HEAD_DIM ** -0.5)
    ko_ref[...] = k * lax.rsqrt(_dot_exact_rhs(k * k, hmean) + NORM_EPS) * kg_ref[...]


def _sb_prep(proj, qg, kg, hmean, tm):
    n = proj.shape[0]
    qblk = RWKV_COLS // PAIR
    kblk = (RWKV_COLS + SB_WIDTH) // PAIR
    assert qblk * PAIR == RWKV_COLS and kblk * PAIR == RWKV_COLS + SB_WIDTH
    const = lambda i, p: (0, 0)
    out = jax.ShapeDtypeStruct((n, SB_WIDTH), F32)
    return pl.pallas_call(
        _sb_prep_kernel,
        grid=(n // tm, SB_PAIRS),
        in_specs=[pl.BlockSpec((tm, PAIR), lambda i, p: (i, qblk + p)),
                  pl.BlockSpec((tm, PAIR), lambda i, p: (i, kblk + p)),
                  pl.BlockSpec((1, PAIR), const), pl.BlockSpec((1, PAIR), const),
                  pl.BlockSpec((PAIR, PAIR), const)],
        out_specs=[pl.BlockSpec((tm, PAIR), lambda i, p: (i, p))] * 2,
        out_shape=[out, out],
        compiler_params=_cparams("parallel", "parallel"),
        name="sb_prep",
    )(proj, proj, qg, kg, hmean)


def _sb_attn_kernel(tq, q_ref, k_ref, v_ref, upper_ref, o_ref):
    qi = pl.program_id(2)
    q = q_ref[...]
    lane_lo = lax.broadcasted_iota(jnp.int32, (tq, PAIR), 1) < HEAD_DIM
    qh = (jnp.where(lane_lo, q, 0.0), jnp.where(lane_lo, 0.0, q))
    upper = upper_ref[...]
    ti = lax.broadcasted_iota(jnp.int32, (tq, tq), 0)
    si = lax.broadcasted_iota(jnp.int32, (tq, tq), 1)
    causal = si < ti

    def block(j, carry, masked):
        kb = k_ref[pl.ds(pl.multiple_of(j * tq, tq), tq), :]
        vb = v_ref[pl.ds(pl.multiple_of(j * tq, tq), tq), :].astype(BF16)
        out = []
        for h in range(2):
            run, acc = carry[h]
            z = _dot_hp(qh[h], kb, NT)
            sp = _softplus(z)
            log_not = -sp
            if masked:
                log_not = jnp.where(causal, log_not, 0.0)
            rem = _dot_exact_rhs(log_not, upper)
            w = jnp.exp((z - sp) + (run + rem))
            if masked:
                w = jnp.where(causal, w, 0.0)
            acc = acc + _dot(w.astype(BF16), vb)
            run = run + jnp.sum(log_not, axis=1, keepdims=True)
            out.append((run, acc))
        return tuple(out)

    zero = (jnp.zeros((tq, 1), F32), jnp.zeros((tq, PAIR), F32))
    carry = block(qi, (zero, zero), True)
    carry = lax.fori_loop(0, qi, lambda i, c: block(qi - 1 - i, c, False), carry)
    o_ref[...] = jnp.where(lane_lo, carry[0][1], carry[1][1])


def _sb_attn(qn, kn, proj, upper, batch, seq_len, tq):
    n = qn.shape[0]
    steps = seq_len // tq
    vblk = (RWKV_COLS + 2 * SB_WIDTH) // PAIR
    assert vblk * PAIR == RWKV_COLS + 2 * SB_WIDTH
    return pl.pallas_call(
        functools.partial(_sb_attn_kernel, tq),
        grid=(batch, SB_PAIRS, steps),
        in_specs=[pl.BlockSpec((tq, PAIR), lambda b, p, i: (b * steps + i, p)),
                  pl.BlockSpec((seq_len, PAIR), lambda b, p, i: (b, p)),
                  pl.BlockSpec((seq_len, PAIR), lambda b, p, i: (b, vblk + p)),
                  pl.BlockSpec((tq, tq), lambda b, p, i: (0, 0))],
        out_specs=pl.BlockSpec((tq, PAIR), lambda b, p, i: (b * steps + i, p)),
        out_shape=jax.ShapeDtypeStruct((n, SB_WIDTH), F32),
        compiler_params=_cparams("parallel", "parallel", "arbitrary"),
        name="sb_attn",
    )(qn, kn, proj, upper)


def _pool_kernel(blocks_per_seq, seq_len, u_ref, halo_ref, w_ref, b_ref, s_ref, o_ref):
    u = u_ref[...]
    tm = u.shape[0]
    blk = pl.program_id(0) % blocks_per_seq
    halo = jnp.where(blk == 0, 0.0, halo_ref[...])
    ext = jnp.concatenate([halo, u], axis=0)
    pos = (blk * tm + lax.broadcasted_iota(jnp.int32, (tm, 1), 0)).astype(F32)
    group = lax.broadcasted_iota(jnp.int32, (tm, POOL_WIDTH), 1) // POOL_GROUP_DIM
    acc = ext
    span = 1
    pooled = jnp.zeros((tm, POOL_WIDTH), F32)
    for gi, win in enumerate(POOL_WINDOWS):
        while span < win:
            acc = acc + pltpu.roll(acc, span, axis=0)
            span *= 2
        count = jnp.minimum(pos + 1.0, float(win))
        pooled = jnp.where(group == gi, acc[POOL_HALO:] / count, pooled)
    pooled = pooled - u
    o_ref[...] = (_dot(pooled.astype(BF16), w_ref[...]) + b_ref[...]) * s_ref[...]


def _pool(proj, w_bd, bias, scale, seq_len, tm):
    n = proj.shape[0]
    ublk = (RWKV_COLS + SB_COLS) // POOL_WIDTH
    assert ublk * POOL_WIDTH == RWKV_COLS + SB_COLS
    const = lambda i: (0, 0)
    halo_idx = lambda i: (jnp.maximum(i * (tm // POOL_HALO) - 1, 0), ublk)
    return pl.pallas_call(
        functools.partial(_pool_kernel, seq_len // tm, seq_len),
        grid=(n // tm,),
        in_specs=[pl.BlockSpec((tm, POOL_WIDTH), lambda i: (i, ublk)),
                  pl.BlockSpec((POOL_HALO, POOL_WIDTH), halo_idx),
                  pl.BlockSpec((POOL_WIDTH, POOL_WIDTH), const),
                  pl.BlockSpec((1, POOL_WIDTH), const), pl.BlockSpec((1, POOL_WIDTH), const)],
        out_specs=pl.BlockSpec((tm, POOL_WIDTH), lambda i: (i, 0)),
        out_shape=jax.ShapeDtypeStruct((n, POOL_WIDTH), F32),
        compiler_params=_cparams("parallel"),
        name="pool",
    )(proj, proj, w_bd, bias, scale)


def _outproj_kernel(x_ref, yr_ref, ys_ref, yp_ref, wr_ref, ws_ref, wp_ref, o_ref):
    acc = _dot(yr_ref[...].astype(BF16), wr_ref[...])
    acc = acc + _dot(ys_ref[...].astype(BF16), ws_ref[...])
    acc = acc + _dot(yp_ref[...].astype(BF16), wp_ref[...])
    o_ref[...] = x_ref[...] + acc


def _outproj(x2, yr, ys, yp, w_bf16, tm):
    n = x2.shape[0]
    row = lambda i: (i, 0)
    const = lambda i: (0, 0)
    wr = w_bf16[:RWKV_WIDTH]
    ws = w_bf16[RWKV_WIDTH:RWKV_WIDTH + SB_WIDTH]
    wp = w_bf16[RWKV_WIDTH + SB_WIDTH:]
    return pl.pallas_call(
        _outproj_kernel,
        grid=(n // tm,),
        in_specs=[pl.BlockSpec((tm, D_MODEL), row),
                  pl.BlockSpec((tm, RWKV_WIDTH), row), pl.BlockSpec((tm, SB_WIDTH), row),
                  pl.BlockSpec((tm, POOL_WIDTH), row),
                  pl.BlockSpec((RWKV_WIDTH, D_MODEL), const), pl.BlockSpec((SB_WIDTH, D_MODEL), const),
                  pl.BlockSpec((POOL_WIDTH, D_MODEL), const)],
        out_specs=pl.BlockSpec((tm, D_MODEL), row),
        out_shape=jax.ShapeDtypeStruct((n, D_MODEL), F32),
        compiler_params=_cparams("parallel"),
        name="outproj",
    )(x2, yr, ys, yp, wr, ws, wp)


def _ffn_kernel(blocks_per_seq, x_ref, halo_ref, g_ref, wg_ref, wv_ref, cwg_ref, cwv_ref, cbg_ref, cbv_ref,
                wd_ref, o_ref, h_ref, acc_ref):
    j = pl.program_id(1)
    tm = x_ref.shape[0]

    @pl.when(j == 0)
    def _():
        first = (pl.program_id(0) % blocks_per_seq) == 0
        g = g_ref[...]
        halo = jnp.where(first, 0.0, _rms_rows(halo_ref[...], g))
        h_ref[0:SUBLANES, :] = halo.astype(BF16)
        h_ref[SUBLANES:, :] = _rms_rows(x_ref[...], g).astype(BF16)
        acc_ref[...] = jnp.zeros_like(acc_ref)

    h = h_ref[...]

    def conv(w_ref, cw_ref, cb_ref):
        up = _dot(h, w_ref[...])
        cw = cw_ref[...]
        c = cb_ref[...] + up[SUBLANES:] * cw[2:3]
        c = c + pltpu.roll(up, 1, axis=0)[SUBLANES:] * cw[1:2]
        return c + pltpu.roll(up, 2, axis=0)[SUBLANES:] * cw[0:1]

    gate = conv(wg_ref, cwg_ref, cbg_ref)
    val = conv(wv_ref, cwv_ref, cbv_ref)
    act = gate * _sigmoid(gate) * val
    acc_ref[...] += _dot(act.astype(BF16), wd_ref[...])

    @pl.when(j == pl.num_programs(1) - 1)
    def _():
        o_ref[...] = x_ref[...] + acc_ref[...]


def _ffn(x2, g, w_up_bf16, conv_w, conv_b, w_down_bf16, seq_len, tm, tf):
    n = x2.shape[0]
    nf = D_FF // tf
    halo_idx = lambda i, j: (jnp.maximum(i * (tm // SUBLANES) - 1, 0), 0)
    conv_b2 = conv_b.reshape(1, 2 * D_FF)
    return pl.pallas_call(
        functools.partial(_ffn_kernel, seq_len // tm),
        grid=(n // tm, nf),
        in_specs=[pl.BlockSpec((tm, D_MODEL), lambda i, j: (i, 0)),
                  pl.BlockSpec((SUBLANES, D_MODEL), halo_idx),
                  pl.BlockSpec((1, D_MODEL), lambda i, j: (0, 0)),
                  pl.BlockSpec((D_MODEL, tf), lambda i, j: (0, j)),
                  pl.BlockSpec((D_MODEL, tf), lambda i, j: (0, nf + j)),
                  pl.BlockSpec((3, tf), lambda i, j: (0, j)),
                  pl.BlockSpec((3, tf), lambda i, j: (0, nf + j)),
                  pl.BlockSpec((1, tf), lambda i, j: (0, j)),
                  pl.BlockSpec((1, tf), lambda i, j: (0, nf + j)),
                  pl.BlockSpec((tf, D_MODEL), lambda i, j: (j, 0))],
        out_specs=pl.BlockSpec((tm, D_MODEL), lambda i, j: (i, 0)),
        out_shape=jax.ShapeDtypeStruct((n, D_MODEL), F32),
        scratch_shapes=[pltpu.VMEM((tm + SUBLANES, D_MODEL), BF16),
                        pltpu.VMEM((tm, D_MODEL), F32)],
        compiler_params=_cparams("parallel", "arbitrary"),
        name="ffn",
    )(x2, x2, g, w_up_bf16, w_up_bf16, conv_w, conv_w, conv_b2, conv_b2, w_down_bf16)


def _head_indicator(width, value):
    idx = jnp.arange(width) // HEAD_DIM
    return jnp.where(idx[:, None] == idx[None, :], value, 0.0).astype(BF16)


def kernel(x, ln1_g, w_in, mu_shift, w0, w2, a0, a2, g2, k_k, k_a, r_k, lnx_w, lnx_b, v0, v1, v2, q_gain, k_gain, pool_w, pool_b, pool_scale, w_out, ln2_g, w_up, conv_w, conv_b, w_down):
    batch, seq_len, _ = x.shape
    depth = w_in.shape[0]
    n = batch * seq_len
    tm = min(512, seq_len)
    tq = min(128, seq_len)
    scan_chunks = min(8, seq_len // CHUNK)
    xform_chunks = min(2, seq_len // CHUNK)
    assert seq_len % tm == 0 and seq_len % (scan_chunks * CHUNK) == 0 and tm % POOL_HALO == 0

    hsum = _head_indicator(RWKV_WIDTH, 1.0)
    hmean = _head_indicator(RWKV_WIDTH, 1.0 / HEAD_DIM)
    tri = jnp.tril(jnp.ones((CHUNK, CHUNK), F32)).astype(BF16)
    upper = jnp.tril(jnp.ones((tq, tq), F32), -1).astype(BF16)
    row = lambda a: a.reshape(1, -1)

    x2 = x.reshape(n, D_MODEL)
    v_first = None
    for l in range(depth):
        proj = _inproj(x2, row(ln1_g[l]), w_in[l].astype(BF16), tm, IN_COLS // 2)

        params = (row(mu_shift[l]), row(w0[l]), w2[l], row(a0[l]), a2[l], g2[l],
                  row(k_k[l]), row(k_a[l]), row(r_k[l]), hsum)
        vmix = None if l == 0 else (v_first, row(v0[l - 1]), v1[l - 1], v2[l - 1])
        r, lw, k, v, kk, a, gate, bonus = _rwkv_prep(proj, seq_len, tm, params, vmix)
        if l == 0:
            v_first = v
        q1, q2, g_mat, e_mat = _rwkv_chunk(r, lw, k, v, kk, a, tri, xform_chunks)
        y_rwkv = _rwkv_scan(q1, q2, g_mat, e_mat, bonus, gate, hmean, row(lnx_w[l]), row(lnx_b[l]),
                            batch, seq_len, scan_chunks)

        qn, kn = _sb_prep(proj, row(jnp.tile(q_gain[l], PAIR // HEAD_DIM)),
                          row(jnp.tile(k_gain[l], PAIR // HEAD_DIM)), hmean[:PAIR, :PAIR], tm)
        y_sb = _sb_attn(qn, kn, proj, upper, batch, seq_len, tq)

        w_bd = jax.scipy.linalg.block_diag(*[pool_w[l, gi] for gi in range(len(POOL_WINDOWS))])
        y_pool = _pool(proj, w_bd.astype(BF16), row(pool_b[l]), row(pool_scale[l]), seq_len, tm)

        x2 = _outproj(x2, y_rwkv, y_sb, y_pool, w_out[l].astype(BF16), tm)
        x2 = _ffn(x2, row(ln2_g[l]), w_up[l].astype(BF16), conv_w[l], conv_b[l], w_down[l].astype(BF16),
                  seq_len, tm, 256)
    return x2.reshape(batch, seq_len, D_MODEL)
```

```python
import functools

import jax
import jax.numpy as jnp
from jax import lax
from jax.experimental import pallas as pl
from jax.experimental.pallas import tpu as pltpu

F32 = jnp.float32
BF16 = jnp.bfloat16

D_MODEL = 1024
HEAD_DIM = 64
RWKV_HEADS = 6
RWKV_WIDTH = RWKV_HEADS * HEAD_DIM
SB_HEADS = 6
SB_WIDTH = SB_HEADS * HEAD_DIM
POOL_WINDOWS = (2, 4, 8, 16)
POOL_WIDTH = D_MODEL - RWKV_WIDTH - SB_WIDTH
POOL_GROUP_DIM = POOL_WIDTH // len(POOL_WINDOWS)
DECAY_LORA = 64
AAA_LORA = 64
GATE_LORA = 128
MV_LORA = 32
RWKV_COLS = 3 * RWKV_WIDTH + DECAY_LORA + AAA_LORA + GATE_LORA
SB_COLS = 3 * SB_WIDTH
IN_COLS = RWKV_COLS + SB_COLS + POOL_WIDTH
D_FF = 2816
NORM_EPS = 1e-6
LN_X_EPS = 64e-5
L2_EPS = 1e-12
LOG2_E = 1.4426950408889634

SUBLANES = 8
LANES = 128
PAIR = 2 * HEAD_DIM
RWKV_PAIRS = RWKV_WIDTH // PAIR
SB_PAIRS = SB_WIDTH // PAIR
CHUNK = 64
POOL_HALO = 16
VMEM_LIMIT = 48 * 1024 * 1024

NN = (((1,), (0,)), ((), ()))
NT = (((1,), (1,)), ((), ()))
TN = (((0,), (0,)), ((), ()))


def _cparams(*sem):
    return pltpu.CompilerParams(dimension_semantics=sem, vmem_limit_bytes=VMEM_LIMIT)


def _dot(a, b, dims=NN):
    return lax.dot_general(a, b, dims, preferred_element_type=F32)


def _split2(x):
    hi = x.astype(BF16)
    lo = (x - hi.astype(F32)).astype(BF16)
    return hi, lo


def _split3(x):
    hi = x.astype(BF16)
    r1 = x - hi.astype(F32)
    mid = r1.astype(BF16)
    lo = (r1 - mid.astype(F32)).astype(BF16)
    return hi, mid, lo


def _dot_hp(a, b, dims=NN):
    ah, al = _split2(a)
    bh, bl = _split2(b)
    return _dot(ah, bh, dims) + (_dot(al, bh, dims) + _dot(ah, bl, dims))


def _dot_exact_rhs(a, b_bf16, dims=NN):
    h, m, l = _split3(a)
    return _dot(h, b_bf16, dims) + (_dot(m, b_bf16, dims) + _dot(l, b_bf16, dims))


def _dot_exact_lhs(a_bf16, b, dims=NN):
    h, m, l = _split3(b)
    return _dot(a_bf16, h, dims) + (_dot(a_bf16, m, dims) + _dot(a_bf16, l, dims))


def _sigmoid(x):
    return 1.0 / (1.0 + jnp.exp(-x))


def _softplus(x):
    return jnp.maximum(x, 0.0) + jnp.log(1.0 + jnp.exp(-jnp.abs(x)))


def _rms_rows(x, g):
    return x * lax.rsqrt(jnp.mean(x * x, axis=-1, keepdims=True) + NORM_EPS) * g


def _inproj_kernel(x_ref, g_ref, w_ref, o_ref):
    h = _rms_rows(x_ref[...], g_ref[...])
    o_ref[...] = _dot(h.astype(BF16), w_ref[...])


def _inproj(x2, g, w_bf16, tm, tn):
    n = x2.shape[0]
    return pl.pallas_call(
        _inproj_kernel,
        grid=(n // tm, IN_COLS // tn),
        in_specs=[pl.BlockSpec((tm, D_MODEL), lambda i, j: (i, 0)),
                  pl.BlockSpec((1, D_MODEL), lambda i, j: (0, 0)),
                  pl.BlockSpec((D_MODEL, tn), lambda i, j: (0, j))],
        out_specs=pl.BlockSpec((tm, tn), lambda i, j: (i, j)),
        out_shape=jax.ShapeDtypeStruct((n, IN_COLS), F32),
        compiler_params=_cparams("parallel", "arbitrary"),
        name="inproj",
    )(x2, g, w_bf16)


def _rwkv_prep_kernel(blocks_per_seq, has_vmix, *refs):
    if has_vmix:
        (p_ref, halo_ref, mu_ref, w0_ref, w2_ref, a0_ref, a2_ref, g2_ref, kk_ref, ka_ref, rk_ref,
         hsum_ref, vf_ref, v0_ref, v1_ref, v2_ref,
         r_o, lw_o, k_o, v_o, kk_o, a_o, g_o, bonus_o) = refs
    else:
        (p_ref, halo_ref, mu_ref, w0_ref, w2_ref, a0_ref, a2_ref, g2_ref, kk_ref, ka_ref, rk_ref,
         hsum_ref,
         r_o, lw_o, k_o, v_o, kk_o, a_o, g_o, bonus_o) = refs
    p = p_ref[...]
    tm = p.shape[0]
    first = (pl.program_id(0) % blocks_per_seq) == 0
    prev_last = jnp.where(first, 0.0, halo_ref[SUBLANES - 1:SUBLANES, :])
    row = lax.broadcasted_iota(jnp.int32, (tm, 1), 0)
    shifted = jnp.where(row == 0, prev_last, pltpu.roll(p, 1, axis=0))
    p = p + (shifted - p) * mu_ref[...]

    c0, c1, c2 = RWKV_WIDTH, 2 * RWKV_WIDTH, 3 * RWKV_WIDTH
    r = p[:, 0:c0]
    k = p[:, c0:c1]
    v = p[:, c1:c2]
    xwa = p[:, c2:c2 + DECAY_LORA + AAA_LORA]
    xw = xwa[:, :DECAY_LORA]
    xa = xwa[:, DECAY_LORA:]
    xg = p[:, c2 + DECAY_LORA + AAA_LORA:]

    w = -_softplus(-(w0_ref[...] + _dot_hp(jnp.tanh(xw), w2_ref[...]))) - 0.5
    lw_o[...] = -jnp.exp(w)
    a = _sigmoid(a0_ref[...] + _dot_hp(xa, a2_ref[...]))
    g_o[...] = _dot_hp(_sigmoid(xg), g2_ref[...])
    if has_vmix:
        mix = _sigmoid(v0_ref[...] + _dot_hp(_dot_hp(v, v1_ref[...]), v2_ref[...]))
        v = v + (vf_ref[...] - v) * mix
    hsum = hsum_ref[...]
    kk = k * kk_ref[...]
    kk = kk / jnp.maximum(jnp.sqrt(_dot_exact_rhs(kk * kk, hsum)), L2_EPS)
    k = k * (1.0 + (a - 1.0) * ka_ref[...])
    r_o[...] = r
    k_o[...] = k
    v_o[...] = v
    kk_o[...] = kk
    a_o[...] = a
    bonus_o[...] = _dot_exact_rhs(r * k * rk_ref[...], hsum) * v


def _rwkv_prep(proj, seq_len, tm, params, vmix):
    n = proj.shape[0]
    has_vmix = vmix is not None
    row = lambda i: (i, 0)
    const = lambda i: (0, 0)
    halo_idx = lambda i: (jnp.maximum(i * (tm // SUBLANES) - 1, 0), 0)
    vec = pl.BlockSpec((1, RWKV_WIDTH), const)
    in_specs = [pl.BlockSpec((tm, RWKV_COLS), row),
                pl.BlockSpec((SUBLANES, RWKV_COLS), halo_idx),
                pl.BlockSpec((1, RWKV_COLS), const),
                vec, pl.BlockSpec((DECAY_LORA, RWKV_WIDTH), const),
                vec, pl.BlockSpec((AAA_LORA, RWKV_WIDTH), const),
                pl.BlockSpec((GATE_LORA, RWKV_WIDTH), const),
                vec, vec, vec,
                pl.BlockSpec((RWKV_WIDTH, RWKV_WIDTH), const)]
    args = [proj, proj] + list(params)
    if has_vmix:
        in_specs += [pl.BlockSpec((tm, RWKV_WIDTH), row), vec,
                     pl.BlockSpec((RWKV_WIDTH, MV_LORA), const),
                     pl.BlockSpec((MV_LORA, RWKV_WIDTH), const)]
        args += list(vmix)
    out = jax.ShapeDtypeStruct((n, RWKV_WIDTH), F32)
    return pl.pallas_call(
        functools.partial(_rwkv_prep_kernel, seq_len // tm, has_vmix),
        grid=(n // tm,),
        in_specs=in_specs,
        out_specs=[pl.BlockSpec((tm, RWKV_WIDTH), row)] * 8,
        out_shape=[out] * 8,
        compiler_params=_cparams("parallel"),
        name="rwkv_prep",
    )(*args)


def _pair_rows(x, lane_lo):
    return jnp.concatenate([jnp.where(lane_lo, x, 0.0), jnp.where(lane_lo, 0.0, x)], axis=0)


def _rwkv_chunk_kernel(chunks, r_ref, lw_ref, k_ref, v_ref, kk_ref, a_ref, tri_ref,
                       q1_o, q2_o, g_o, e_o):
    two = 2 * CHUNK
    ri = lax.broadcasted_iota(jnp.int32, (two, two), 0)
    ci = lax.broadcasted_iota(jnp.int32, (two, two), 1)
    same = (ri // CHUNK) == (ci // CHUNK)
    strict = same & ((ci % CHUNK) < (ri % CHUNK))
    incl = same & ((ci % CHUNK) <= (ri % CHUNK))
    eye = (ri == ci).astype(F32)
    lane_lo = lax.broadcasted_iota(jnp.int32, (CHUNK, PAIR), 1) < HEAD_DIM
    tri = tri_ref[...]

    units = []
    for c in range(chunks):
        rows = pl.ds(c * CHUNK, CHUNK)
        lw = lw_ref[rows, :]
        cl = _dot_exact_lhs(tri, lw)
        e_pos = jnp.exp(cl)
        e_prev = jnp.exp(cl - lw)
        e_neg = jnp.exp(-cl)
        kk = kk_ref[rows, :]
        at = -kk * e_prev
        bt = kk * a_ref[rows, :] * e_neg
        rt = r_ref[rows, :] * e_pos
        kt = k_ref[rows, :] * e_neg
        v = v_ref[rows, :]
        wc = e_pos[CHUNK - 1:CHUNK, :]
        for p in range(RWKV_PAIRS):
            ls = slice(p * PAIR, (p + 1) * PAIR)
            units.append(dict(
                c=c, p=p, rows=rows, ls=ls, wc=wc[:, ls],
                xa=_pair_rows(at[:, ls], lane_lo).astype(BF16), xr=_pair_rows(rt[:, ls], lane_lo),
                vv=_pair_rows(v[:, ls], lane_lo).astype(BF16), v=v[:, ls].astype(BF16),
                bt=bt[:, ls].astype(BF16), kt=kt[:, ls].astype(BF16)))

    for u in units:
        sc = _dot(jnp.concatenate([u["xa"], u["xr"].astype(BF16)], axis=0),
                  jnp.concatenate([u["bt"], u["bt"], u["kt"], u["kt"]], axis=0), NT)
        a_ab = jnp.where(strict, sc[:two, :two], 0.0)
        u["a_ak"] = jnp.where(strict, sc[:two, two:], 0.0).astype(BF16)
        u["m"] = jnp.concatenate([jnp.where(incl, sc[two:, :two], 0.0),
                                  jnp.where(incl, sc[two:, two:], 0.0)], axis=1).astype(BF16)
        u["t"] = eye + a_ab
        u["pw"] = a_ab.astype(BF16)

    for _ in range(CHUNK.bit_length() - 2):
        for u in units:
            u["pw"] = _dot(u["pw"], u["pw"]).astype(BF16)
        for u in units:
            u["t"] = u["t"] + _dot(u["t"].astype(BF16), u["pw"])

    for u in units:
        u["z"] = _dot(u["a_ak"], u["vv"]).astype(BF16)
    for u in units:
        u["pm"] = _dot(u["t"].astype(BF16), jnp.concatenate([u["xa"], u["z"]], axis=1))
    for u in units:
        vv = u["vv"]
        qm = _dot(u["m"], jnp.concatenate([u["pm"].astype(BF16),
                                           jnp.concatenate([jnp.zeros_like(vv), vv], axis=1)], axis=0))
        q1m = u["xr"] + qm[:, :PAIR]
        q2m = qm[:, PAIR:]
        q1_o[u["rows"], u["ls"]] = q1m[:CHUNK] + q1m[CHUNK:]
        q2_o[u["rows"], u["ls"]] = q2m[:CHUNK] + q2m[CHUNK:]
    for u in units:
        pm = u["pm"]
        p12 = (pm[:CHUNK] + pm[CHUNK:]).astype(BF16)
        lt = jnp.concatenate([p12, jnp.concatenate([jnp.zeros_like(u["bt"]), u["v"]], axis=1)], axis=0)
        ge = _dot(lt, jnp.concatenate([u["bt"], u["kt"]], axis=0), TN)
        g_o[u["c"], u["p"]] = jnp.where(same, (eye + ge[:PAIR]) * u["wc"], 0.0)
        e_o[u["c"], u["p"]] = jnp.where(same, ge[PAIR:] * u["wc"], 0.0)


def _rwkv_chunk(r, lw, k, v, kk, a, tri, chunks):
    n = r.shape[0]
    tc = chunks * CHUNK
    row = lambda i: (i, 0)
    tok = pl.BlockSpec((tc, RWKV_WIDTH), row)
    mat = pl.BlockSpec((chunks, RWKV_PAIRS, PAIR, PAIR), lambda i: (i, 0, 0, 0))
    tok_shape = jax.ShapeDtypeStruct((n, RWKV_WIDTH), F32)
    mat_shape = jax.ShapeDtypeStruct((n // CHUNK, RWKV_PAIRS, PAIR, PAIR), F32)
    return pl.pallas_call(
        functools.partial(_rwkv_chunk_kernel, chunks),
        grid=(n // tc,),
        in_specs=[tok] * 6 + [pl.BlockSpec((CHUNK, CHUNK), lambda i: (0, 0))],
        out_specs=[tok, tok, mat, mat],
        out_shape=[tok_shape, tok_shape, mat_shape, mat_shape],
        compiler_params=_cparams("parallel"),
        name="rwkv_chunk",
    )(r, lw, k, v, kk, a, tri)


def _rwkv_scan_kernel(chunks, q1_ref, q2_ref, g_ref, e_ref, bonus_ref, gate_ref, hmean_ref,
                      lnw_ref, lnb_ref, o_ref, s_ref, y_ref):
    @pl.when(pl.program_id(1) == 0)
    def _():
        s_ref[...] = jnp.zeros_like(s_ref)

    for c in range(chunks):
        rows = pl.ds(c * CHUNK, CHUNK)
        for p in range(RWKV_PAIRS):
            ls = slice(p * PAIR, (p + 1) * PAIR)
            s = s_ref[p]
            y_ref[rows, ls] = _dot_hp(q1_ref[rows, ls], s, NT) + q2_ref[rows, ls]
            s_ref[p] = _dot_hp(s, g_ref[c, p]) + e_ref[c, p]

    y = y_ref[...]
    hmean = hmean_ref[...]
    mean = _dot_exact_rhs(y, hmean)
    yc = y - mean
    var = _dot_exact_rhs(yc * yc, hmean)
    yn = yc * lax.rsqrt(var + LN_X_EPS) * lnw_ref[...] + lnb_ref[...]
    o_ref[...] = (yn + bonus_ref[...]) * gate_ref[...]


def _rwkv_scan(q1, q2, g, e, bonus, gate, hmean, lnw, lnb, batch, seq_len, chunks):
    n = q1.shape[0]
    tc = chunks * CHUNK
    steps = seq_len // tc
    row = lambda b, i: (b * steps + i, 0)
    const = lambda b, i: (0, 0)
    tok = pl.BlockSpec((tc, RWKV_WIDTH), row)
    mat = pl.BlockSpec((chunks, RWKV_PAIRS, PAIR, PAIR), lambda b, i: (b * steps + i, 0, 0, 0))
    vec = pl.BlockSpec((1, RWKV_WIDTH), const)
    return pl.pallas_call(
        functools.partial(_rwkv_scan_kernel, chunks),
        grid=(batch, steps),
        in_specs=[tok, tok, mat, mat, tok, tok,
                  pl.BlockSpec((RWKV_WIDTH, RWKV_WIDTH), const), vec, vec],
        out_specs=tok,
        out_shape=jax.ShapeDtypeStruct((n, RWKV_WIDTH), F32),
        scratch_shapes=[pltpu.VMEM((RWKV_PAIRS, PAIR, PAIR), F32),
                        pltpu.VMEM((tc, RWKV_WIDTH), F32)],
        compiler_params=_cparams("parallel", "arbitrary"),
        name="rwkv_scan",
    )(q1, q2, g, e, bonus, gate, hmean, lnw, lnb)


def _sb_prep_kernel(q_ref, k_ref, v_ref, qg_ref, kg_ref, hmean_ref, q_o, k_o, v_o):
    hmean = hmean_ref[...]
    q = q_ref[...]
    k = k_ref[...]
    qn = q * lax.rsqrt(_dot_exact_rhs(q * q, hmean) + NORM_EPS) * qg_ref[...] * (LOG2_E * HEAD_DIM ** -0.5)
    kn = k * lax.rsqrt(_dot_exact_rhs(k * k, hmean) + NORM_EPS) * kg_ref[...]
    lane_lo = lax.broadcasted_iota(jnp.int32, qn.shape, 1) < HEAD_DIM
    qm = jnp.concatenate([jnp.where(lane_lo, qn, 0.0), jnp.where(lane_lo, 0.0, qn)], axis=1)
    q_o[...] = qm.astype(BF16)
    k_o[...] = kn.astype(BF16)
    v_o[...] = v_ref[...].astype(BF16)


def _sb_prep(proj, qg, kg, hmean, tm):
    n = proj.shape[0]
    qblk = RWKV_COLS // PAIR
    kblk = (RWKV_COLS + SB_WIDTH) // PAIR
    vblk = (RWKV_COLS + 2 * SB_WIDTH) // PAIR
    assert qblk * PAIR == RWKV_COLS
    const = lambda i, p: (0, 0)
    q_shape = jax.ShapeDtypeStruct((n, 2 * SB_WIDTH), BF16)
    k_shape = jax.ShapeDtypeStruct((n, SB_WIDTH), BF16)
    q_spec = pl.BlockSpec((tm, 2 * PAIR), lambda i, p: (i, p))
    k_spec = pl.BlockSpec((tm, PAIR), lambda i, p: (i, p))
    return pl.pallas_call(
        _sb_prep_kernel,
        grid=(n // tm, SB_PAIRS),
        in_specs=[pl.BlockSpec((tm, PAIR), lambda i, p: (i, qblk + p)),
                  pl.BlockSpec((tm, PAIR), lambda i, p: (i, kblk + p)),
                  pl.BlockSpec((tm, PAIR), lambda i, p: (i, vblk + p)),
                  pl.BlockSpec((1, PAIR), const), pl.BlockSpec((1, PAIR), const),
                  pl.BlockSpec((PAIR, PAIR), const)],
        out_specs=[q_spec, k_spec, k_spec],
        out_shape=[q_shape, k_shape, k_shape],
        compiler_params=_cparams("parallel", "parallel"),
        name="sb_prep",
    )(proj, proj, proj, qg, kg, hmean)


def _sb_attn_kernel(tq, q_ref, k_ref, v_ref, upper2_ref, o_ref):
    qi = pl.program_id(2)
    upper2 = upper2_ref[...]
    ti = lax.broadcasted_iota(jnp.int32, (tq, tq), 0)
    si = lax.broadcasted_iota(jnp.int32, (tq, tq), 1)
    causal = si < ti

    def blocks(js, carry, masked):
        tiles = [(h, pl.ds(pl.multiple_of(j * tq, tq), tq)) for j in js for h in range(2)]
        zs = [_dot(q_ref[:, h * PAIR:(h + 1) * PAIR], k_ref[ks, :], NT) for h, ks in tiles]
        sps = []
        for z in zs:
            sp = jnp.maximum(z, 0.0) + jnp.log2(1.0 + jnp.exp2(-jnp.abs(z)))
            sps.append(jnp.where(causal, sp, 0.0) if masked else sp)
        laters = [_dot(jnp.concatenate(_split2(sp), axis=1), upper2) for sp in sps]
        carry = list(carry)
        for (h, ks), z, sp, later in zip(tiles, zs, sps, laters):
            used, acc = carry[h]
            w = jnp.exp2((z - sp) - (later + used))
            if masked:
                w = jnp.where(causal, w, 0.0)
            carry[h] = (used + (later[:, 0:1] + sp[:, 0:1]), acc + _dot(w.astype(BF16), v_ref[ks, :]))
        return tuple(carry)

    zero = (jnp.zeros((tq, 1), F32), jnp.zeros((tq, PAIR), F32))
    carry = blocks([qi], (zero, zero), True)
    carry = lax.fori_loop(0, qi // 2, lambda i, c: blocks([qi - 1 - 2 * i, qi - 2 - 2 * i], c, False), carry)
    carry = lax.cond(qi % 2 == 1, lambda c: blocks([0], c, False), lambda c: c, carry)
    lane_lo = lax.broadcasted_iota(jnp.int32, (tq, PAIR), 1) < HEAD_DIM
    o_ref[...] = jnp.where(lane_lo, carry[0][1], carry[1][1])


def _sb_attn(qm, kn, vb, upper2, batch, seq_len, tq):
    n = qm.shape[0]
    steps = seq_len // tq
    q_spec = pl.BlockSpec((tq, 2 * PAIR), lambda b, p, i: (b * steps + i, p))
    k_spec = pl.BlockSpec((seq_len, PAIR), lambda b, p, i: (b, p))
    return pl.pallas_call(
        functools.partial(_sb_attn_kernel, tq),
        grid=(batch, SB_PAIRS, steps),
        in_specs=[q_spec, k_spec, k_spec,
                  pl.BlockSpec((2 * tq, tq), lambda b, p, i: (0, 0))],
        out_specs=pl.BlockSpec((tq, PAIR), lambda b, p, i: (b * steps + i, p)),
        out_shape=jax.ShapeDtypeStruct((n, SB_WIDTH), F32),
        compiler_params=_cparams("parallel", "parallel", "arbitrary"),
        name="sb_attn",
    )(qm, kn, vb, upper2)


def _pool_kernel(blocks_per_seq, seq_len, u_ref, halo_ref, w_ref, b_ref, s_ref, o_ref):
    u = u_ref[...]
    tm = u.shape[0]
    blk = pl.program_id(0) % blocks_per_seq
    halo = jnp.where(blk == 0, 0.0, halo_ref[...])
    ext = jnp.concatenate([halo, u], axis=0)
    pos = (blk * tm + lax.broadcasted_iota(jnp.int32, (tm, 1), 0)).astype(F32)
    group = lax.broadcasted_iota(jnp.int32, (tm, POOL_WIDTH), 1) // POOL_GROUP_DIM
    acc = ext
    span = 1
    pooled = jnp.zeros((tm, POOL_WIDTH), F32)
    for gi, win in enumerate(POOL_WINDOWS):
        while span < win:
            acc = acc + pltpu.roll(acc, span, axis=0)
            span *= 2
        count = jnp.minimum(pos + 1.0, float(win))
        pooled = jnp.where(group == gi, acc[POOL_HALO:] / count, pooled)
    pooled = pooled - u
    o_ref[...] = (_dot(pooled.astype(BF16), w_ref[...]) + b_ref[...]) * s_ref[...]


def _pool(proj, w_bd, bias, scale, seq_len, tm):
    n = proj.shape[0]
    ublk = (RWKV_COLS + SB_COLS) // POOL_WIDTH
    assert ublk * POOL_WIDTH == RWKV_COLS + SB_COLS
    const = lambda i: (0, 0)
    halo_idx = lambda i: (jnp.maximum(i * (tm // POOL_HALO) - 1, 0), ublk)
    return pl.pallas_call(
        functools.partial(_pool_kernel, seq_len // tm, seq_len),
        grid=(n // tm,),
        in_specs=[pl.BlockSpec((tm, POOL_WIDTH), lambda i: (i, ublk)),
                  pl.BlockSpec((POOL_HALO, POOL_WIDTH), halo_idx),
                  pl.BlockSpec((POOL_WIDTH, POOL_WIDTH), const),
                  pl.BlockSpec((1, POOL_WIDTH), const), pl.BlockSpec((1, POOL_WIDTH), const)],
        out_specs=pl.BlockSpec((tm, POOL_WIDTH), lambda i: (i, 0)),
        out_shape=jax.ShapeDtypeStruct((n, POOL_WIDTH), F32),
        compiler_params=_cparams("parallel"),
        name="pool",
    )(proj, proj, w_bd, bias, scale)


def _outproj_kernel(x_ref, yr_ref, ys_ref, yp_ref, wr_ref, ws_ref, wp_ref, o_ref):
    acc = _dot(yr_ref[...].astype(BF16), wr_ref[...])
    acc = acc + _dot(ys_ref[...].astype(BF16), ws_ref[...])
    acc = acc + _dot(yp_ref[...].astype(BF16), wp_ref[...])
    o_ref[...] = x_ref[...] + acc


def _outproj(x2, yr, ys, yp, w_bf16, tm):
    n = x2.shape[0]
    row = lambda i: (i, 0)
    const = lambda i: (0, 0)
    wr = w_bf16[:RWKV_WIDTH]
    ws = w_bf16[RWKV_WIDTH:RWKV_WIDTH + SB_WIDTH]
    wp = w_bf16[RWKV_WIDTH + SB_WIDTH:]
    return pl.pallas_call(
        _outproj_kernel,
        grid=(n // tm,),
        in_specs=[pl.BlockSpec((tm, D_MODEL), row),
                  pl.BlockSpec((tm, RWKV_WIDTH), row), pl.BlockSpec((tm, SB_WIDTH), row),
                  pl.BlockSpec((tm, POOL_WIDTH), row),
                  pl.BlockSpec((RWKV_WIDTH, D_MODEL), const), pl.BlockSpec((SB_WIDTH, D_MODEL), const),
                  pl.BlockSpec((POOL_WIDTH, D_MODEL), const)],
        out_specs=pl.BlockSpec((tm, D_MODEL), row),
        out_shape=jax.ShapeDtypeStruct((n, D_MODEL), F32),
        compiler_params=_cparams("parallel"),
        name="outproj",
    )(x2, yr, ys, yp, wr, ws, wp)


def _ffn_kernel(blocks_per_seq, x_ref, halo_ref, g_ref, wg_ref, wv_ref, cwg_ref, cwv_ref, cbg_ref, cbv_ref,
                wd_ref, o_ref, h_ref, acc_ref):
    j = pl.program_id(1)
    tm = x_ref.shape[0]

    @pl.when(j == 0)
    def _():
        first = (pl.program_id(0) % blocks_per_seq) == 0
        g = g_ref[...]
        halo = jnp.where(first, 0.0, _rms_rows(halo_ref[...], g))
        h_ref[0:SUBLANES, :] = halo.astype(BF16)
        h_ref[SUBLANES:, :] = _rms_rows(x_ref[...], g).astype(BF16)
        acc_ref[...] = jnp.zeros_like(acc_ref)

    h = h_ref[...]

    def conv(w_ref, cw_ref, cb_ref):
        up = _dot(h, w_ref[...])
        cw = cw_ref[...]
        c = cb_ref[...] + up[SUBLANES:] * cw[2:3]
        c = c + pltpu.roll(up, 1, axis=0)[SUBLANES:] * cw[1:2]
        return c + pltpu.roll(up, 2, axis=0)[SUBLANES:] * cw[0:1]

    gate = conv(wg_ref, cwg_ref, cbg_ref)
    val = conv(wv_ref, cwv_ref, cbv_ref)
    act = gate * _sigmoid(gate) * val
    acc_ref[...] += _dot(act.astype(BF16), wd_ref[...])

    @pl.when(j == pl.num_programs(1) - 1)
    def _():
        o_ref[...] = x_ref[...] + acc_ref[...]


def _ffn(x2, g, w_up_bf16, conv_w, conv_b, w_down_bf16, seq_len, tm, tf):
    n = x2.shape[0]
    nf = D_FF // tf
    halo_idx = lambda i, j: (jnp.maximum(i * (tm // SUBLANES) - 1, 0), 0)
    conv_b2 = conv_b.reshape(1, 2 * D_FF)
    return pl.pallas_call(
        functools.partial(_ffn_kernel, seq_len // tm),
        grid=(n // tm, nf),
        in_specs=[pl.BlockSpec((tm, D_MODEL), lambda i, j: (i, 0)),
                  pl.BlockSpec((SUBLANES, D_MODEL), halo_idx),
                  pl.BlockSpec((1, D_MODEL), lambda i, j: (0, 0)),
                  pl.BlockSpec((D_MODEL, tf), lambda i, j: (0, j)),
                  pl.BlockSpec((D_MODEL, tf), lambda i, j: (0, nf + j)),
                  pl.BlockSpec((3, tf), lambda i, j: (0, j)),
                  pl.BlockSpec((3, tf), lambda i, j: (0, nf + j)),
                  pl.BlockSpec((1, tf), lambda i, j: (0, j)),
                  pl.BlockSpec((1, tf), lambda i, j: (0, nf + j)),
                  pl.BlockSpec((tf, D_MODEL), lambda i, j: (j, 0))],
        out_specs=pl.BlockSpec((tm, D_MODEL), lambda i, j: (i, 0)),
        out_shape=jax.ShapeDtypeStruct((n, D_MODEL), F32),
        scratch_shapes=[pltpu.VMEM((tm + SUBLANES, D_MODEL), BF16),
                        pltpu.VMEM((tm, D_MODEL), F32)],
        compiler_params=_cparams("parallel", "arbitrary"),
        name="ffn",
    )(x2, x2, g, w_up_bf16, w_up_bf16, conv_w, conv_w, conv_b2, conv_b2, w_down_bf16)


def _head_indicator(width, value):
    idx = jnp.arange(width) // HEAD_DIM
    return jnp.where(idx[:, None] == idx[None, :], value, 0.0).astype(BF16)


def kernel(x, ln1_g, w_in, mu_shift, w0, w2, a0, a2, g2, k_k, k_a, r_k, lnx_w, lnx_b, v0, v1, v2, q_gain, k_gain, pool_w, pool_b, pool_scale, w_out, ln2_g, w_up, conv_w, conv_b, w_down):
    batch, seq_len, _ = x.shape
    depth = w_in.shape[0]
    n = batch * seq_len
    tm = min(512, seq_len)
    tq = min(256, seq_len)
    scan_chunks = min(8, seq_len // CHUNK)
    xform_chunks = min(4, seq_len // CHUNK)
    assert seq_len % tm == 0 and seq_len % (scan_chunks * CHUNK) == 0 and tm % POOL_HALO == 0

    hsum = _head_indicator(RWKV_WIDTH, 1.0)
    hmean = _head_indicator(RWKV_WIDTH, 1.0 / HEAD_DIM)
    tri = jnp.tril(jnp.ones((CHUNK, CHUNK), F32)).astype(BF16)
    upper2 = jnp.tile(jnp.tril(jnp.ones((tq, tq), F32), -1), (2, 1)).astype(BF16)
    row = lambda a: a.reshape(1, -1)

    x2 = x.reshape(n, D_MODEL)
    v_first = None
    for l in range(depth):
        proj = _inproj(x2, row(ln1_g[l]), w_in[l].astype(BF16), tm, IN_COLS // 2)

        params = (row(mu_shift[l]), row(w0[l]), w2[l], row(a0[l]), a2[l], g2[l],
                  row(k_k[l]), row(k_a[l]), row(r_k[l]), hsum)
        vmix = None if l == 0 else (v_first, row(v0[l - 1]), v1[l - 1], v2[l - 1])
        r, lw, k, v, kk, a, gate, bonus = _rwkv_prep(proj, seq_len, tm, params, vmix)
        if l == 0:
            v_first = v
        q1, q2, g_mat, e_mat = _rwkv_chunk(r, lw, k, v, kk, a, tri, xform_chunks)
        y_rwkv = _rwkv_scan(q1, q2, g_mat, e_mat, bonus, gate, hmean, row(lnx_w[l]), row(lnx_b[l]),
                            batch, seq_len, scan_chunks)

        sb_in = _sb_prep(proj, row(jnp.tile(q_gain[l], PAIR // HEAD_DIM)),
                         row(jnp.tile(k_gain[l], PAIR // HEAD_DIM)), hmean[:PAIR, :PAIR], tm)
        y_sb = _sb_attn(*sb_in, upper2, batch, seq_len, tq)

        w_bd = jax.scipy.linalg.block_diag(*[pool_w[l, gi] for gi in range(len(POOL_WINDOWS))])
        y_pool = _pool(proj, w_bd.astype(BF16), row(pool_b[l]), row(pool_scale[l]), seq_len, tm)

        x2 = _outproj(x2, y_rwkv, y_sb, y_pool, w_out[l].astype(BF16), tm)
        x2 = _ffn(x2, row(ln2_g[l]), w_up[l].astype(BF16), conv_w[l], conv_b[l], w_down[l].astype(BF16),
                  seq_len, tm, 256)
    return x2.reshape(batch, seq_len, D_MODEL)
```

```python
import functools

import jax
import jax.numpy as jnp
from jax import lax
from jax.experimental import pallas as pl
from jax.experimental.pallas import tpu as pltpu

F32 = jnp.float32
BF16 = jnp.bfloat16

D_MODEL = 1024
HEAD_DIM = 64
RWKV_HEADS = 6
RWKV_WIDTH = RWKV_HEADS * HEAD_DIM
SB_HEADS = 6
SB_WIDTH = SB_HEADS * HEAD_DIM
POOL_WINDOWS = (2, 4, 8, 16)
POOL_WIDTH = D_MODEL - RWKV_WIDTH - SB_WIDTH
POOL_GROUP_DIM = POOL_WIDTH // len(POOL_WINDOWS)
DECAY_LORA = 64
AAA_LORA = 64
GATE_LORA = 128
MV_LORA = 32
RWKV_COLS = 3 * RWKV_WIDTH + DECAY_LORA + AAA_LORA + GATE_LORA
SB_COLS = 3 * SB_WIDTH
IN_COLS = RWKV_COLS + SB_COLS + POOL_WIDTH
D_FF = 2816
NORM_EPS = 1e-6
LN_X_EPS = 64e-5
L2_EPS = 1e-12
LOG2_E = 1.4426950408889634

SUBLANES = 8
LANES = 128
PAIR = 2 * HEAD_DIM
RWKV_PAIRS = RWKV_WIDTH // PAIR
SB_PAIRS = SB_WIDTH // PAIR
CHUNK = 64
POOL_HALO = 16
KEY_BLOCKS = 2
VMEM_LIMIT = 48 * 1024 * 1024

NN = (((1,), (0,)), ((), ()))
NT = (((1,), (1,)), ((), ()))
TN = (((0,), (0,)), ((), ()))


def _cparams(*sem):
    return pltpu.CompilerParams(dimension_semantics=sem, vmem_limit_bytes=VMEM_LIMIT)


def _dot(a, b, dims=NN):
    return lax.dot_general(a, b, dims, preferred_element_type=F32)


def _split2(x):
    hi = x.astype(BF16)
    lo = (x - hi.astype(F32)).astype(BF16)
    return hi, lo


def _split2_trunc(x):
    hi = lax.bitcast_convert_type(lax.bitcast_convert_type(x, jnp.int32) & jnp.int32(-65536), F32)
    return hi.astype(BF16), (x - hi).astype(BF16)


def _neg_abs(x):
    return lax.bitcast_convert_type(lax.bitcast_convert_type(x, jnp.int32) | jnp.int32(-2 ** 31), F32)


def _split3(x):
    hi = x.astype(BF16)
    r1 = x - hi.astype(F32)
    mid = r1.astype(BF16)
    lo = (r1 - mid.astype(F32)).astype(BF16)
    return hi, mid, lo


def _dot_hp(a, b, dims=NN):
    ah, al = _split2(a)
    bh, bl = _split2(b)
    return _dot(ah, bh, dims) + (_dot(al, bh, dims) + _dot(ah, bl, dims))


def _dot_exact_rhs(a, b_bf16, dims=NN):
    h, m, l = _split3(a)
    return _dot(h, b_bf16, dims) + (_dot(m, b_bf16, dims) + _dot(l, b_bf16, dims))


def _dot_exact_lhs(a_bf16, b, dims=NN):
    h, m, l = _split3(b)
    return _dot(a_bf16, h, dims) + (_dot(a_bf16, m, dims) + _dot(a_bf16, l, dims))


def _sigmoid(x):
    return 1.0 / (1.0 + jnp.exp(-x))


def _softplus(x):
    return jnp.maximum(x, 0.0) + jnp.log(1.0 + jnp.exp(-jnp.abs(x)))


def _rms_rows(x, g):
    return x * lax.rsqrt(jnp.mean(x * x, axis=-1, keepdims=True) + NORM_EPS) * g


def _inproj_kernel(x_ref, g_ref, w_ref, o_ref):
    h = _rms_rows(x_ref[...], g_ref[...])
    o_ref[...] = _dot(h.astype(BF16), w_ref[...])


def _inproj(x2, g, w_bf16, tm):
    n = x2.shape[0]
    return pl.pallas_call(
        _inproj_kernel,
        grid=(n // tm,),
        in_specs=[pl.BlockSpec((tm, D_MODEL), lambda i: (i, 0)),
                  pl.BlockSpec((1, D_MODEL), lambda i: (0, 0)),
                  pl.BlockSpec((D_MODEL, IN_COLS), lambda i: (0, 0), pipeline_mode=pl.Buffered(1))],
        out_specs=pl.BlockSpec((tm, IN_COLS), lambda i: (i, 0)),
        out_shape=jax.ShapeDtypeStruct((n, IN_COLS), F32),
        compiler_params=_cparams("parallel"),
        name="inproj",
    )(x2, g, w_bf16)


def _rwkv_prep_kernel(blocks_per_seq, has_vmix, *refs):
    if has_vmix:
        (p_ref, halo_ref, mu_ref, w0_ref, w2_ref, a0_ref, a2_ref, g2_ref, kk_ref, ka_ref, rk_ref,
         hsum_ref, vf_ref, v0_ref, v1_ref, v2_ref,
         r_o, lw_o, k_o, v_o, kk_o, a_o, g_o, bonus_o) = refs
    else:
        (p_ref, halo_ref, mu_ref, w0_ref, w2_ref, a0_ref, a2_ref, g2_ref, kk_ref, ka_ref, rk_ref,
         hsum_ref,
         r_o, lw_o, k_o, v_o, kk_o, a_o, g_o, bonus_o) = refs
    p = p_ref[...]
    tm = p.shape[0]
    first = (pl.program_id(0) % blocks_per_seq) == 0
    prev_last = jnp.where(first, 0.0, halo_ref[SUBLANES - 1:SUBLANES, :])
    row = lax.broadcasted_iota(jnp.int32, (tm, 1), 0)
    shifted = jnp.where(row == 0, prev_last, pltpu.roll(p, 1, axis=0))
    p = p + (shifted - p) * mu_ref[...]

    c0, c1, c2 = RWKV_WIDTH, 2 * RWKV_WIDTH, 3 * RWKV_WIDTH
    r = p[:, 0:c0]
    k = p[:, c0:c1]
    v = p[:, c1:c2]
    xwa = p[:, c2:c2 + DECAY_LORA + AAA_LORA]
    xw = xwa[:, :DECAY_LORA]
    xa = xwa[:, DECAY_LORA:]
    xg = p[:, c2 + DECAY_LORA + AAA_LORA:]

    w = -_softplus(-(w0_ref[...] + _dot_hp(jnp.tanh(xw), w2_ref[...]))) - 0.5
    lw_o[...] = -jnp.exp(w)
    a = _sigmoid(a0_ref[...] + _dot_hp(xa, a2_ref[...]))
    g_o[...] = _dot_hp(_sigmoid(xg), g2_ref[...])
    if has_vmix:
        mix = _sigmoid(v0_ref[...] + _dot_hp(_dot_hp(v, v1_ref[...]), v2_ref[...]))
        v = v + (vf_ref[...] - v) * mix
    hsum = hsum_ref[...]
    kk = k * kk_ref[...]
    kk = kk / jnp.maximum(jnp.sqrt(_dot_exact_rhs(kk * kk, hsum)), L2_EPS)
    k = k * (1.0 + (a - 1.0) * ka_ref[...])
    r_o[...] = r
    k_o[...] = k
    v_o[...] = v
    kk_o[...] = kk
    a_o[...] = a
    bonus_o[...] = _dot_exact_rhs(r * k * rk_ref[...], hsum) * v


def _rwkv_prep(proj, seq_len, tm, params, vmix):
    n = proj.shape[0]
    has_vmix = vmix is not None
    row = lambda i: (i, 0)
    const = lambda i: (0, 0)
    halo_idx = lambda i: (jnp.maximum(i * (tm // SUBLANES) - 1, 0), 0)
    vec = pl.BlockSpec((1, RWKV_WIDTH), const)
    in_specs = [pl.BlockSpec((tm, RWKV_COLS), row),
                pl.BlockSpec((SUBLANES, RWKV_COLS), halo_idx),
                pl.BlockSpec((1, RWKV_COLS), const),
                vec, pl.BlockSpec((DECAY_LORA, RWKV_WIDTH), const),
                vec, pl.BlockSpec((AAA_LORA, RWKV_WIDTH), const),
                pl.BlockSpec((GATE_LORA, RWKV_WIDTH), const),
                vec, vec, vec,
                pl.BlockSpec((RWKV_WIDTH, RWKV_WIDTH), const)]
    args = [proj, proj] + list(params)
    if has_vmix:
        in_specs += [pl.BlockSpec((tm, RWKV_WIDTH), row), vec,
                     pl.BlockSpec((RWKV_WIDTH, MV_LORA), const),
                     pl.BlockSpec((MV_LORA, RWKV_WIDTH), const)]
        args += list(vmix)
    out = jax.ShapeDtypeStruct((n, RWKV_WIDTH), F32)
    return pl.pallas_call(
        functools.partial(_rwkv_prep_kernel, seq_len // tm, has_vmix),
        grid=(n // tm,),
        in_specs=in_specs,
        out_specs=[pl.BlockSpec((tm, RWKV_WIDTH), row)] * 8,
        out_shape=[out] * 8,
        compiler_params=_cparams("parallel"),
        name="rwkv_prep",
    )(*args)


def _pair_rows(x, lane_lo):
    return jnp.concatenate([jnp.where(lane_lo, x, 0.0), jnp.where(lane_lo, 0.0, x)], axis=0)


def _rwkv_chunk_kernel(chunks, r_ref, lw_ref, k_ref, v_ref, kk_ref, a_ref, tri_ref,
                       q1_o, q2_o, g_o, e_o):
    two = 2 * CHUNK
    ri = lax.broadcasted_iota(jnp.int32, (two, two), 0)
    ci = lax.broadcasted_iota(jnp.int32, (two, two), 1)
    same = (ri // CHUNK) == (ci // CHUNK)
    strict = same & ((ci % CHUNK) < (ri % CHUNK))
    incl = same & ((ci % CHUNK) <= (ri % CHUNK))
    eye = (ri == ci).astype(F32)
    lane_lo = lax.broadcasted_iota(jnp.int32, (CHUNK, PAIR), 1) < HEAD_DIM
    tri = tri_ref[...]

    units = []
    for c in range(chunks):
        rows = pl.ds(c * CHUNK, CHUNK)
        lw = lw_ref[rows, :]
        cl = _dot_exact_lhs(tri, lw)
        e_pos = jnp.exp(cl)
        e_prev = jnp.exp(cl - lw)
        e_neg = jnp.exp(-cl)
        kk = kk_ref[rows, :]
        at = -kk * e_prev
        bt = kk * a_ref[rows, :] * e_neg
        rt = r_ref[rows, :] * e_pos
        kt = k_ref[rows, :] * e_neg
        v = v_ref[rows, :]
        wc = e_pos[CHUNK - 1:CHUNK, :]
        for p in range(RWKV_PAIRS):
            ls = slice(p * PAIR, (p + 1) * PAIR)
            units.append(dict(
                c=c, p=p, rows=rows, ls=ls, wc=wc[:, ls],
                xa=_pair_rows(at[:, ls], lane_lo).astype(BF16), xr=_pair_rows(rt[:, ls], lane_lo),
                vv=_pair_rows(v[:, ls], lane_lo).astype(BF16), v=v[:, ls].astype(BF16),
                bt=bt[:, ls].astype(BF16), kt=kt[:, ls].astype(BF16)))

    for u in units:
        sc = _dot(jnp.concatenate([u["xa"], u["xr"].astype(BF16)], axis=0),
                  jnp.concatenate([u["bt"], u["bt"], u["kt"], u["kt"]], axis=0), NT)
        a_ab = jnp.where(strict, sc[:two, :two], 0.0)
        u["a_ak"] = jnp.where(strict, sc[:two, two:], 0.0).astype(BF16)
        u["m"] = jnp.concatenate([jnp.where(incl, sc[two:, :two], 0.0),
                                  jnp.where(incl, sc[two:, two:], 0.0)], axis=1).astype(BF16)
        u["t"] = eye + a_ab
        u["pw"] = a_ab.astype(BF16)

    for _ in range(CHUNK.bit_length() - 2):
        for u in units:
            u["pw"] = _dot(u["pw"], u["pw"]).astype(BF16)
        for u in units:
            u["t"] = u["t"] + _dot(u["t"].astype(BF16), u["pw"])

    for u in units:
        u["z"] = _dot(u["a_ak"], u["vv"]).astype(BF16)
    for u in units:
        u["pm"] = _dot(u["t"].astype(BF16), jnp.concatenate([u["xa"], u["z"]], axis=1))
    for u in units:
        vv = u["vv"]
        qm = _dot(u["m"], jnp.concatenate([u["pm"].astype(BF16),
                                           jnp.concatenate([jnp.zeros_like(vv), vv], axis=1)], axis=0))
        q1m = u["xr"] + qm[:, :PAIR]
        q2m = qm[:, PAIR:]
        q1_o[u["rows"], u["ls"]] = q1m[:CHUNK] + q1m[CHUNK:]
        q2_o[u["rows"], u["ls"]] = q2m[:CHUNK] + q2m[CHUNK:]
    for u in units:
        pm = u["pm"]
        p12 = (pm[:CHUNK] + pm[CHUNK:]).astype(BF16)
        lt = jnp.concatenate([p12, jnp.concatenate([jnp.zeros_like(u["bt"]), u["v"]], axis=1)], axis=0)
        ge = _dot(lt, jnp.concatenate([u["bt"], u["kt"]], axis=0), TN)
        g_o[u["c"], u["p"]] = jnp.where(same, (eye + ge[:PAIR]) * u["wc"], 0.0)
        e_o[u["c"], u["p"]] = jnp.where(same, ge[PAIR:] * u["wc"], 0.0)


def _rwkv_chunk(r, lw, k, v, kk, a, tri, chunks):
    n = r.shape[0]
    tc = chunks * CHUNK
    row = lambda i: (i, 0)
    tok = pl.BlockSpec((tc, RWKV_WIDTH), row)
    mat = pl.BlockSpec((chunks, RWKV_PAIRS, PAIR, PAIR), lambda i: (i, 0, 0, 0))
    tok_shape = jax.ShapeDtypeStruct((n, RWKV_WIDTH), F32)
    mat_shape = jax.ShapeDtypeStruct((n // CHUNK, RWKV_PAIRS, PAIR, PAIR), F32)
    return pl.pallas_call(
        functools.partial(_rwkv_chunk_kernel, chunks),
        grid=(n // tc,),
        in_specs=[tok] * 6 + [pl.BlockSpec((CHUNK, CHUNK), lambda i: (0, 0))],
        out_specs=[tok, tok, mat, mat],
        out_shape=[tok_shape, tok_shape, mat_shape, mat_shape],
        compiler_params=_cparams("parallel"),
        name="rwkv_chunk",
    )(r, lw, k, v, kk, a, tri)


def _rwkv_scan_kernel(chunks, q1_ref, q2_ref, g_ref, e_ref, bonus_ref, gate_ref, hmean_ref,
                      lnw_ref, lnb_ref, o_ref, s_ref, y_ref):
    @pl.when(pl.program_id(1) == 0)
    def _():
        s_ref[...] = jnp.zeros_like(s_ref)

    for c in range(chunks):
        rows = pl.ds(c * CHUNK, CHUNK)
        for p in range(RWKV_PAIRS):
            ls = slice(p * PAIR, (p + 1) * PAIR)
            s = s_ref[p]
            s_hi, s_lo = _split2(s)
            y_ref[rows, ls] = _dot(q1_ref[rows, ls].astype(BF16), s_hi, NT) + q2_ref[rows, ls]
            g = g_ref[c, p].astype(BF16)
            s_ref[p] = _dot(s_hi, g) + (_dot(s_lo, g) + e_ref[c, p])

    y = y_ref[...]
    hmean = hmean_ref[...]
    mean = _dot_exact_rhs(y, hmean)
    yc = y - mean
    var = _dot_exact_rhs(yc * yc, hmean)
    yn = yc * lax.rsqrt(var + LN_X_EPS) * lnw_ref[...] + lnb_ref[...]
    o_ref[...] = (yn + bonus_ref[...]) * gate_ref[...]


def _rwkv_scan(q1, q2, g, e, bonus, gate, hmean, lnw, lnb, batch, seq_len, chunks):
    n = q1.shape[0]
    tc = chunks * CHUNK
    steps = seq_len // tc
    row = lambda b, i: (b * steps + i, 0)
    const = lambda b, i: (0, 0)
    tok = pl.BlockSpec((tc, RWKV_WIDTH), row)
    mat = pl.BlockSpec((chunks, RWKV_PAIRS, PAIR, PAIR), lambda b, i: (b * steps + i, 0, 0, 0))
    vec = pl.BlockSpec((1, RWKV_WIDTH), const)
    return pl.pallas_call(
        functools.partial(_rwkv_scan_kernel, chunks),
        grid=(batch, steps),
        in_specs=[tok, tok, mat, mat, tok, tok,
                  pl.BlockSpec((RWKV_WIDTH, RWKV_WIDTH), const), vec, vec],
        out_specs=tok,
        out_shape=jax.ShapeDtypeStruct((n, RWKV_WIDTH), F32),
        scratch_shapes=[pltpu.VMEM((RWKV_PAIRS, PAIR, PAIR), F32),
                        pltpu.VMEM((tc, RWKV_WIDTH), F32)],
        compiler_params=_cparams("parallel", "arbitrary"),
        name="rwkv_scan",
    )(q1, q2, g, e, bonus, gate, hmean, lnw, lnb)


def _sb_prep_kernel(q_ref, k_ref, v_ref, qg_ref, kg_ref, hmean_ref, q_o, k_o, v_o):
    hmean = hmean_ref[...]
    q = q_ref[...]
    k = k_ref[...]
    qn = q * lax.rsqrt(_dot_exact_rhs(q * q, hmean) + NORM_EPS) * qg_ref[...] * (LOG2_E * HEAD_DIM ** -0.5)
    kn = k * lax.rsqrt(_dot_exact_rhs(k * k, hmean) + NORM_EPS) * kg_ref[...]
    lane_lo = lax.broadcasted_iota(jnp.int32, qn.shape, 1) < HEAD_DIM
    qm = jnp.concatenate([jnp.where(lane_lo, qn, 0.0), jnp.where(lane_lo, 0.0, qn)], axis=1)
    q_o[...] = qm.astype(BF16)
    k_o[...] = kn.astype(BF16)
    v_o[...] = v_ref[...].astype(BF16)


def _sb_prep(proj, qg, kg, hmean, tm):
    n = proj.shape[0]
    qblk = RWKV_COLS // PAIR
    kblk = (RWKV_COLS + SB_WIDTH) // PAIR
    vblk = (RWKV_COLS + 2 * SB_WIDTH) // PAIR
    assert qblk * PAIR == RWKV_COLS
    const = lambda i, p: (0, 0)
    q_shape = jax.ShapeDtypeStruct((n, 2 * SB_WIDTH), BF16)
    k_shape = jax.ShapeDtypeStruct((n, SB_WIDTH), BF16)
    q_spec = pl.BlockSpec((tm, 2 * PAIR), lambda i, p: (i, p))
    k_spec = pl.BlockSpec((tm, PAIR), lambda i, p: (i, p))
    return pl.pallas_call(
        _sb_prep_kernel,
        grid=(n // tm, SB_PAIRS),
        in_specs=[pl.BlockSpec((tm, PAIR), lambda i, p: (i, qblk + p)),
                  pl.BlockSpec((tm, PAIR), lambda i, p: (i, kblk + p)),
                  pl.BlockSpec((tm, PAIR), lambda i, p: (i, vblk + p)),
                  pl.BlockSpec((1, PAIR), const), pl.BlockSpec((1, PAIR), const),
                  pl.BlockSpec((PAIR, PAIR), const)],
        out_specs=[q_spec, k_spec, k_spec],
        out_shape=[q_shape, k_shape, k_shape],
        compiler_params=_cparams("parallel", "parallel"),
        name="sb_prep",
    )(proj, proj, proj, qg, kg, hmean)


def _sb_attn_kernel(tq, q_ref, k_ref, v_ref, upper2_ref, o_ref):
    qi = pl.program_id(2)
    upper2 = upper2_ref[...]
    ti = lax.broadcasted_iota(jnp.int32, (tq, tq), 0)
    si = lax.broadcasted_iota(jnp.int32, (tq, tq), 1)
    causal = si < ti

    def key_rows(j):
        return pl.ds(pl.multiple_of(j * tq, tq), tq)

    def scores(js):
        return [_dot(q_ref[:, h * PAIR:(h + 1) * PAIR], k_ref[key_rows(j), :], NT) for j in js for h in range(2)]

    def consume(js, zs, carry, masked):
        tiles = [(h, key_rows(j)) for j in js for h in range(2)]
        sps = []
        for z in zs:
            sp = jnp.maximum(z, 0.0) + jnp.log2(1.0 + jnp.exp2(_neg_abs(z)))
            sps.append(jnp.where(causal, sp, 0.0) if masked else sp)
        laters = [_dot(jnp.concatenate(_split2_trunc(sp), axis=1), upper2) for sp in sps]
        carry = list(carry)
        for (h, ks), z, sp, later in zip(tiles, zs, sps, laters):
            used, acc = carry[h]
            w = jnp.exp2((z - sp) - (later + used))
            if masked:
                w = jnp.where(causal, w, 0.0)
            carry[h] = (used + (later[:, 0:1] + sp[:, 0:1]), acc + _dot(w.astype(BF16), v_ref[ks, :]))
        return tuple(carry)

    zero = (jnp.zeros((tq, 1), F32), jnp.zeros((tq, PAIR), F32))
    carry = consume([qi], scores([qi]), (zero, zero), True)

    def group(i):
        return [jnp.maximum(qi - 1 - KEY_BLOCKS * i - d, 0) for d in range(KEY_BLOCKS)]

    carry = lax.fori_loop(0, qi // KEY_BLOCKS, lambda i, c: consume(group(i), scores(group(i)), c, False), carry)
    for d in range(KEY_BLOCKS - 1, 0, -1):
        carry = lax.cond(qi % KEY_BLOCKS >= d, lambda c, d=d: consume([d - 1], scores([d - 1]), c, False),
                         lambda c: c, carry)
    lane_lo = lax.broadcasted_iota(jnp.int32, (tq, PAIR), 1) < HEAD_DIM
    o_ref[...] = jnp.where(lane_lo, carry[0][1], carry[1][1])


def _sb_attn(qm, kn, vb, upper2, batch, seq_len, tq):
    n = qm.shape[0]
    steps = seq_len // tq
    q_spec = pl.BlockSpec((tq, 2 * PAIR), lambda b, p, i: (b * steps + i, p))
    k_spec = pl.BlockSpec((seq_len, PAIR), lambda b, p, i: (b, p))
    return pl.pallas_call(
        functools.partial(_sb_attn_kernel, tq),
        grid=(batch, SB_PAIRS, steps),
        in_specs=[q_spec, k_spec, k_spec,
                  pl.BlockSpec((2 * tq, tq), lambda b, p, i: (0, 0))],
        out_specs=pl.BlockSpec((tq, PAIR), lambda b, p, i: (b * steps + i, p)),
        out_shape=jax.ShapeDtypeStruct((n, SB_WIDTH), F32),
        compiler_params=_cparams("parallel", "parallel", "arbitrary"),
        name="sb_attn",
    )(qm, kn, vb, upper2)


def _pool_kernel(blocks_per_seq, seq_len, u_ref, halo_ref, w_ref, b_ref, s_ref, o_ref):
    u = u_ref[...]
    tm = u.shape[0]
    blk = pl.program_id(0) % blocks_per_seq
    halo = jnp.where(blk == 0, 0.0, halo_ref[...])
    ext = jnp.concatenate([halo, u], axis=0)
    pos = (blk * tm + lax.broadcasted_iota(jnp.int32, (tm, 1), 0)).astype(F32)
    group = lax.broadcasted_iota(jnp.int32, (tm, POOL_WIDTH), 1) // POOL_GROUP_DIM
    acc = ext
    span = 1
    pooled = jnp.zeros((tm, POOL_WIDTH), F32)
    for gi, win in enumerate(POOL_WINDOWS):
        while span < win:
            acc = acc + pltpu.roll(acc, span, axis=0)
            span *= 2
        count = jnp.minimum(pos + 1.0, float(win))
        pooled = jnp.where(group == gi, acc[POOL_HALO:] / count, pooled)
    pooled = pooled - u
    o_ref[...] = (_dot(pooled.astype(BF16), w_ref[...]) + b_ref[...]) * s_ref[...]


def _pool(proj, w_bd, bias, scale, seq_len, tm):
    n = proj.shape[0]
    ublk = (RWKV_COLS + SB_COLS) // POOL_WIDTH
    assert ublk * POOL_WIDTH == RWKV_COLS + SB_COLS
    const = lambda i: (0, 0)
    halo_idx = lambda i: (jnp.maximum(i * (tm // POOL_HALO) - 1, 0), ublk)
    return pl.pallas_call(
        functools.partial(_pool_kernel, seq_len // tm, seq_len),
        grid=(n // tm,),
        in_specs=[pl.BlockSpec((tm, POOL_WIDTH), lambda i: (i, ublk)),
                  pl.BlockSpec((POOL_HALO, POOL_WIDTH), halo_idx),
                  pl.BlockSpec((POOL_WIDTH, POOL_WIDTH), const),
                  pl.BlockSpec((1, POOL_WIDTH), const), pl.BlockSpec((1, POOL_WIDTH), const)],
        out_specs=pl.BlockSpec((tm, POOL_WIDTH), lambda i: (i, 0)),
        out_shape=jax.ShapeDtypeStruct((n, POOL_WIDTH), F32),
        compiler_params=_cparams("parallel"),
        name="pool",
    )(proj, proj, w_bd, bias, scale)


def _outproj_kernel(x_ref, yr_ref, ys_ref, yp_ref, wr_ref, ws_ref, wp_ref, o_ref):
    acc = _dot(yr_ref[...].astype(BF16), wr_ref[...])
    acc = acc + _dot(ys_ref[...].astype(BF16), ws_ref[...])
    acc = acc + _dot(yp_ref[...].astype(BF16), wp_ref[...])
    o_ref[...] = x_ref[...] + acc


def _outproj(x2, yr, ys, yp, w_bf16, tm):
    n = x2.shape[0]
    row = lambda i: (i, 0)
    const = lambda i: (0, 0)
    wr = w_bf16[:RWKV_WIDTH]
    ws = w_bf16[RWKV_WIDTH:RWKV_WIDTH + SB_WIDTH]
    wp = w_bf16[RWKV_WIDTH + SB_WIDTH:]
    return pl.pallas_call(
        _outproj_kernel,
        grid=(n // tm,),
        in_specs=[pl.BlockSpec((tm, D_MODEL), row),
                  pl.BlockSpec((tm, RWKV_WIDTH), row), pl.BlockSpec((tm, SB_WIDTH), row),
                  pl.BlockSpec((tm, POOL_WIDTH), row),
                  pl.BlockSpec((RWKV_WIDTH, D_MODEL), const), pl.BlockSpec((SB_WIDTH, D_MODEL), const),
                  pl.BlockSpec((POOL_WIDTH, D_MODEL), const)],
        out_specs=pl.BlockSpec((tm, D_MODEL), row),
        out_shape=jax.ShapeDtypeStruct((n, D_MODEL), F32),
        compiler_params=_cparams("parallel"),
        name="outproj",
    )(x2, yr, ys, yp, wr, ws, wp)


def _ffn_kernel(blocks_per_seq, tf, x_ref, halo_ref, g_ref, wup_ref, cw_ref, cb_ref, wd_ref, o_ref, h_ref, act_ref):
    first = (pl.program_id(0) % blocks_per_seq) == 0
    g = g_ref[...]
    x = x_ref[...]
    h_ref[0:SUBLANES, :] = jnp.where(first, 0.0, _rms_rows(halo_ref[...], g)).astype(BF16)
    h_ref[SUBLANES:, :] = _rms_rows(x, g).astype(BF16)
    h = h_ref[...]

    def conv(col):
        cols = slice(col, col + tf)
        up = _dot(h, wup_ref[:, cols])
        cw = cw_ref[:, cols]
        c = cb_ref[:, cols] + up[SUBLANES:] * cw[2:3]
        c = c + pltpu.roll(up, 1, axis=0)[SUBLANES:] * cw[1:2]
        return c + pltpu.roll(up, 2, axis=0)[SUBLANES:] * cw[0:1]

    for j in range(D_FF // tf):
        gate = conv(j * tf)
        val = conv(D_FF + j * tf)
        act_ref[:, j * tf:(j + 1) * tf] = (gate * _sigmoid(gate) * val).astype(BF16)
    o_ref[...] = x + _dot(act_ref[...], wd_ref[...])


def _ffn(x2, g, w_up_bf16, conv_w, conv_b, w_down_bf16, seq_len, tm, tf):
    n = x2.shape[0]
    const = lambda i: (0, 0)
    halo_idx = lambda i: (jnp.maximum(i * (tm // SUBLANES) - 1, 0), 0)
    resident = lambda shape: pl.BlockSpec(shape, const, pipeline_mode=pl.Buffered(1))
    return pl.pallas_call(
        functools.partial(_ffn_kernel, seq_len // tm, tf),
        grid=(n // tm,),
        in_specs=[pl.BlockSpec((tm, D_MODEL), lambda i: (i, 0)),
                  pl.BlockSpec((SUBLANES, D_MODEL), halo_idx),
                  pl.BlockSpec((1, D_MODEL), const),
                  resident((D_MODEL, 2 * D_FF)),
                  pl.BlockSpec((3, 2 * D_FF), const),
                  pl.BlockSpec((1, 2 * D_FF), const),
                  resident((D_FF, D_MODEL))],
        out_specs=pl.BlockSpec((tm, D_MODEL), lambda i: (i, 0)),
        out_shape=jax.ShapeDtypeStruct((n, D_MODEL), F32),
        scratch_shapes=[pltpu.VMEM((tm + SUBLANES, D_MODEL), BF16),
                        pltpu.VMEM((tm, D_FF), BF16)],
        compiler_params=_cparams("parallel"),
        name="ffn",
    )(x2, x2, g, w_up_bf16, conv_w, conv_b.reshape(1, 2 * D_FF), w_down_bf16)


def _head_indicator(width, value):
    idx = jnp.arange(width) // HEAD_DIM
    return jnp.where(idx[:, None] == idx[None, :], value, 0.0).astype(BF16)


def kernel(x, ln1_g, w_in, mu_shift, w0, w2, a0, a2, g2, k_k, k_a, r_k, lnx_w, lnx_b, v0, v1, v2, q_gain, k_gain, pool_w, pool_b, pool_scale, w_out, ln2_g, w_up, conv_w, conv_b, w_down):
    batch, seq_len, _ = x.shape
    depth = w_in.shape[0]
    n = batch * seq_len
    tm = min(512, seq_len)
    tq = min(256, seq_len)
    scan_chunks = min(8, seq_len // CHUNK)
    xform_chunks = min(4, seq_len // CHUNK)
    assert seq_len % tm == 0 and seq_len % (scan_chunks * CHUNK) == 0 and tm % POOL_HALO == 0

    hsum = _head_indicator(RWKV_WIDTH, 1.0)
    hmean = _head_indicator(RWKV_WIDTH, 1.0 / HEAD_DIM)
    tri = jnp.tril(jnp.ones((CHUNK, CHUNK), F32)).astype(BF16)
    upper2 = jnp.tile(jnp.tril(jnp.ones((tq, tq), F32), -1), (2, 1)).astype(BF16)
    row = lambda a: a.reshape(1, -1)

    x2 = x.reshape(n, D_MODEL)
    v_first = None
    for l in range(depth):
        proj = _inproj(x2, row(ln1_g[l]), w_in[l].astype(BF16), tm)

        params = (row(mu_shift[l]), row(w0[l]), w2[l], row(a0[l]), a2[l], g2[l],
                  row(k_k[l]), row(k_a[l]), row(r_k[l]), hsum)
        vmix = None if l == 0 else (v_first, row(v0[l - 1]), v1[l - 1], v2[l - 1])
        r, lw, k, v, kk, a, gate, bonus = _rwkv_prep(proj, seq_len, tm, params, vmix)
        if l == 0:
            v_first = v
        q1, q2, g_mat, e_mat = _rwkv_chunk(r, lw, k, v, kk, a, tri, xform_chunks)
        y_rwkv = _rwkv_scan(q1, q2, g_mat, e_mat, bonus, gate, hmean, row(lnx_w[l]), row(lnx_b[l]),
                            batch, seq_len, scan_chunks)

        sb_in = _sb_prep(proj, row(jnp.tile(q_gain[l], PAIR // HEAD_DIM)),
                         row(jnp.tile(k_gain[l], PAIR // HEAD_DIM)), hmean[:PAIR, :PAIR], tm)
        y_sb = _sb_attn(*sb_in, upper2, batch, seq_len, tq)

        w_bd = jax.scipy.linalg.block_diag(*[pool_w[l, gi] for gi in range(len(POOL_WINDOWS))])
        y_pool = _pool(proj, w_bd.astype(BF16), row(pool_b[l]), row(pool_scale[l]), seq_len, tm)

        x2 = _outproj(x2, y_rwkv, y_sb, y_pool, w_out[l].astype(BF16), tm)
        x2 = _ffn(x2, row(ln2_g[l]), w_up[l].astype(BF16), conv_w[l], conv_b[l], w_down[l].astype(BF16),
                  seq_len, tm, 256)
    return x2.reshape(batch, seq_len, D_MODEL)
```

```python
import functools

import jax
import jax.numpy as jnp
from jax import lax
from jax.experimental import pallas as pl
from jax.experimental.pallas import tpu as pltpu

F32 = jnp.float32
BF16 = jnp.bfloat16

D_MODEL = 1024
HEAD_DIM = 64
RWKV_HEADS = 6
RWKV_WIDTH = RWKV_HEADS * HEAD_DIM
SB_HEADS = 6
SB_WIDTH = SB_HEADS * HEAD_DIM
POOL_WINDOWS = (2, 4, 8, 16)
POOL_WIDTH = D_MODEL - RWKV_WIDTH - SB_WIDTH
POOL_GROUP_DIM = POOL_WIDTH // len(POOL_WINDOWS)
DECAY_LORA = 64
AAA_LORA = 64
GATE_LORA = 128
MV_LORA = 32
RWKV_COLS = 3 * RWKV_WIDTH + DECAY_LORA + AAA_LORA + GATE_LORA
SB_COLS = 3 * SB_WIDTH
IN_COLS = RWKV_COLS + SB_COLS + POOL_WIDTH
D_FF = 2816
NORM_EPS = 1e-6
LN_X_EPS = 64e-5
L2_EPS = 1e-12
LOG2_E = 1.4426950408889634

SUBLANES = 8
LANES = 128
BF16_ROWS = 2 * SUBLANES
PAIR = 2 * HEAD_DIM
RWKV_PAIRS = RWKV_WIDTH // PAIR
SB_PAIRS = SB_WIDTH // PAIR
CHUNK = 64
POOL_HALO = 16
KEY_BLOCKS = 4
VMEM_LIMIT = 48 * 1024 * 1024

NN = (((1,), (0,)), ((), ()))
NT = (((1,), (1,)), ((), ()))
TN = (((0,), (0,)), ((), ()))


def _cparams(*sem):
    return pltpu.CompilerParams(dimension_semantics=sem, vmem_limit_bytes=VMEM_LIMIT)


def _dot(a, b, dims=NN):
    return lax.dot_general(a, b, dims, preferred_element_type=F32)


def _split2(x):
    hi = x.astype(BF16)
    lo = (x - hi.astype(F32)).astype(BF16)
    return hi, lo


def _neg_abs(x):
    return lax.bitcast_convert_type(lax.bitcast_convert_type(x, jnp.int32) | jnp.int32(-2 ** 31), F32)


def _split3(x):
    hi = x.astype(BF16)
    r1 = x - hi.astype(F32)
    mid = r1.astype(BF16)
    lo = (r1 - mid.astype(F32)).astype(BF16)
    return hi, mid, lo


def _dot_hp(a, b, dims=NN):
    ah, al = _split2(a)
    bh, bl = _split2(b)
    return _dot(ah, bh, dims) + (_dot(al, bh, dims) + _dot(ah, bl, dims))


def _dot_exact_rhs(a, b_bf16, dims=NN):
    h, m, l = _split3(a)
    return _dot(h, b_bf16, dims) + (_dot(m, b_bf16, dims) + _dot(l, b_bf16, dims))


def _dot_exact_lhs(a_bf16, b, dims=NN):
    h, m, l = _split3(b)
    return _dot(a_bf16, h, dims) + (_dot(a_bf16, m, dims) + _dot(a_bf16, l, dims))


def _sigmoid(x):
    return 1.0 / (1.0 + jnp.exp(-x))


def _softplus(x):
    return jnp.maximum(x, 0.0) + jnp.log(1.0 + jnp.exp(-jnp.abs(x)))


def _rms_rows(x, g):
    return x * lax.rsqrt(jnp.mean(x * x, axis=-1, keepdims=True) + NORM_EPS) * g


def _inproj_kernel(x_ref, g_ref, w_ref, o_ref):
    h = _rms_rows(x_ref[...], g_ref[...])
    o_ref[...] = _dot(h.astype(BF16), w_ref[...])


def _inproj(x2, g, w_bf16, tm):
    n = x2.shape[0]
    return pl.pallas_call(
        _inproj_kernel,
        grid=(n // tm,),
        in_specs=[pl.BlockSpec((tm, D_MODEL), lambda i: (i, 0)),
                  pl.BlockSpec((1, D_MODEL), lambda i: (0, 0)),
                  pl.BlockSpec((D_MODEL, IN_COLS), lambda i: (0, 0), pipeline_mode=pl.Buffered(1))],
        out_specs=pl.BlockSpec((tm, IN_COLS), lambda i: (i, 0)),
        out_shape=jax.ShapeDtypeStruct((n, IN_COLS), F32),
        compiler_params=_cparams("parallel"),
        name="inproj",
    )(x2, g, w_bf16)


def _rwkv_prep_kernel(blocks_per_seq, has_vmix, *refs):
    if has_vmix:
        (p_ref, halo_ref, mu_ref, w0_ref, w2_ref, a0_ref, a2_ref, g2_ref, kk_ref, ka_ref, rk_ref,
         hsum_ref, vf_ref, v0_ref, v1_ref, v2_ref,
         r_o, lw_o, k_o, v_o, kk_o, a_o, g_o, bonus_o) = refs
    else:
        (p_ref, halo_ref, mu_ref, w0_ref, w2_ref, a0_ref, a2_ref, g2_ref, kk_ref, ka_ref, rk_ref,
         hsum_ref,
         r_o, lw_o, k_o, v_o, kk_o, a_o, g_o, bonus_o) = refs
    p = p_ref[...]
    tm = p.shape[0]
    first = (pl.program_id(0) % blocks_per_seq) == 0
    prev_last = jnp.where(first, 0.0, halo_ref[SUBLANES - 1:SUBLANES, :])
    row = lax.broadcasted_iota(jnp.int32, (tm, 1), 0)
    shifted = jnp.where(row == 0, prev_last, pltpu.roll(p, 1, axis=0))
    p = p + (shifted - p) * mu_ref[...]

    c0, c1, c2 = RWKV_WIDTH, 2 * RWKV_WIDTH, 3 * RWKV_WIDTH
    r = p[:, 0:c0]
    k = p[:, c0:c1]
    v = p[:, c1:c2]
    xwa = p[:, c2:c2 + DECAY_LORA + AAA_LORA]
    xw = xwa[:, :DECAY_LORA]
    xa = xwa[:, DECAY_LORA:]
    xg = p[:, c2 + DECAY_LORA + AAA_LORA:]

    w = -_softplus(-(w0_ref[...] + _dot_hp(jnp.tanh(xw), w2_ref[...]))) - 0.5
    lw_o[...] = -jnp.exp(w)
    a = _sigmoid(a0_ref[...] + _dot_hp(xa, a2_ref[...]))
    g_o[...] = _dot_hp(_sigmoid(xg), g2_ref[...])
    if has_vmix:
        mix = _sigmoid(v0_ref[...] + _dot_hp(_dot_hp(v, v1_ref[...]), v2_ref[...]))
        v = v + (vf_ref[...] - v) * mix
    hsum = hsum_ref[...]
    kk = k * kk_ref[...]
    kk = kk / jnp.maximum(jnp.sqrt(_dot_exact_rhs(kk * kk, hsum)), L2_EPS)
    k = k * (1.0 + (a - 1.0) * ka_ref[...])
    r_o[...] = r
    k_o[...] = k
    v_o[...] = v
    kk_o[...] = kk
    a_o[...] = a
    bonus_o[...] = _dot_exact_rhs(r * k * rk_ref[...], hsum) * v


def _rwkv_prep(proj, seq_len, tm, params, vmix):
    n = proj.shape[0]
    has_vmix = vmix is not None
    row = lambda i: (i, 0)
    const = lambda i: (0, 0)
    halo_idx = lambda i: (jnp.maximum(i * (tm // SUBLANES) - 1, 0), 0)
    vec = pl.BlockSpec((1, RWKV_WIDTH), const)
    in_specs = [pl.BlockSpec((tm, RWKV_COLS), row),
                pl.BlockSpec((SUBLANES, RWKV_COLS), halo_idx),
                pl.BlockSpec((1, RWKV_COLS), const),
                vec, pl.BlockSpec((DECAY_LORA, RWKV_WIDTH), const),
                vec, pl.BlockSpec((AAA_LORA, RWKV_WIDTH), const),
                pl.BlockSpec((GATE_LORA, RWKV_WIDTH), const),
                vec, vec, vec,
                pl.BlockSpec((RWKV_WIDTH, RWKV_WIDTH), const)]
    args = [proj, proj] + list(params)
    if has_vmix:
        in_specs += [pl.BlockSpec((tm, RWKV_WIDTH), row), vec,
                     pl.BlockSpec((RWKV_WIDTH, MV_LORA), const),
                     pl.BlockSpec((MV_LORA, RWKV_WIDTH), const)]
        args += list(vmix)
    out = jax.ShapeDtypeStruct((n, RWKV_WIDTH), F32)
    return pl.pallas_call(
        functools.partial(_rwkv_prep_kernel, seq_len // tm, has_vmix),
        grid=(n // tm,),
        in_specs=in_specs,
        out_specs=[pl.BlockSpec((tm, RWKV_WIDTH), row)] * 8,
        out_shape=[out] * 8,
        compiler_params=_cparams("parallel"),
        name="rwkv_prep",
    )(*args)


def _pair_rows(x, lane_lo):
    return jnp.concatenate([jnp.where(lane_lo, x, 0.0), jnp.where(lane_lo, 0.0, x)], axis=0)


def _rwkv_chunk_kernel(chunks, r_ref, lw_ref, k_ref, v_ref, kk_ref, a_ref, tri_ref,
                       q1_o, q2_o, g_o, e_o):
    two = 2 * CHUNK
    ri = lax.broadcasted_iota(jnp.int32, (two, two), 0)
    ci = lax.broadcasted_iota(jnp.int32, (two, two), 1)
    same = (ri // CHUNK) == (ci // CHUNK)
    strict = same & ((ci % CHUNK) < (ri % CHUNK))
    incl = same & ((ci % CHUNK) <= (ri % CHUNK))
    eye = (ri == ci).astype(F32)
    lane_lo = lax.broadcasted_iota(jnp.int32, (CHUNK, PAIR), 1) < HEAD_DIM
    tri = tri_ref[...]

    units = []
    for c in range(chunks):
        rows = pl.ds(c * CHUNK, CHUNK)
        lw = lw_ref[rows, :]
        cl = _dot_exact_lhs(tri, lw)
        e_pos = jnp.exp(cl)
        e_prev = jnp.exp(cl - lw)
        e_neg = jnp.exp(-cl)
        kk = kk_ref[rows, :]
        at = -kk * e_prev
        bt = kk * a_ref[rows, :] * e_neg
        rt = r_ref[rows, :] * e_pos
        kt = k_ref[rows, :] * e_neg
        v = v_ref[rows, :]
        wc = e_pos[CHUNK - 1:CHUNK, :]
        for p in range(RWKV_PAIRS):
            ls = slice(p * PAIR, (p + 1) * PAIR)
            units.append(dict(
                c=c, p=p, rows=rows, ls=ls, wc=wc[:, ls],
                xa=_pair_rows(at[:, ls], lane_lo).astype(BF16), xr=_pair_rows(rt[:, ls], lane_lo),
                vv=_pair_rows(v[:, ls], lane_lo).astype(BF16), v=v[:, ls].astype(BF16),
                bt=bt[:, ls].astype(BF16), kt=kt[:, ls].astype(BF16)))

    for u in units:
        sc = _dot(jnp.concatenate([u["xa"], u["xr"].astype(BF16)], axis=0),
                  jnp.concatenate([u["bt"], u["bt"], u["kt"], u["kt"]], axis=0), NT)
        a_ab = jnp.where(strict, sc[:two, :two], 0.0)
        u["a_ak"] = jnp.where(strict, sc[:two, two:], 0.0).astype(BF16)
        u["m"] = jnp.concatenate([jnp.where(incl, sc[two:, :two], 0.0),
                                  jnp.where(incl, sc[two:, two:], 0.0)], axis=1).astype(BF16)
        u["t"] = eye + a_ab
        u["pw"] = a_ab.astype(BF16)

    for _ in range(CHUNK.bit_length() - 2):
        for u in units:
            u["pw"] = _dot(u["pw"], u["pw"]).astype(BF16)
        for u in units:
            u["t"] = u["t"] + _dot(u["t"].astype(BF16), u["pw"])

    for u in units:
        u["z"] = _dot(u["a_ak"], u["vv"]).astype(BF16)
    for u in units:
        u["pm"] = _dot(u["t"].astype(BF16), jnp.concatenate([u["xa"], u["z"]], axis=1))
    for u in units:
        vv = u["vv"]
        qm = _dot(u["m"], jnp.concatenate([u["pm"].astype(BF16),
                                           jnp.concatenate([jnp.zeros_like(vv), vv], axis=1)], axis=0))
        q1m = u["xr"] + qm[:, :PAIR]
        q2m = qm[:, PAIR:]
        q1_o[u["rows"], u["ls"]] = (q1m[:CHUNK] + q1m[CHUNK:]).astype(BF16)
        q2_o[u["rows"], u["ls"]] = q2m[:CHUNK] + q2m[CHUNK:]
    for u in units:
        pm = u["pm"]
        p12 = (pm[:CHUNK] + pm[CHUNK:]).astype(BF16)
        lt = jnp.concatenate([p12, jnp.concatenate([jnp.zeros_like(u["bt"]), u["v"]], axis=1)], axis=0)
        ge = _dot(lt, jnp.concatenate([u["bt"], u["kt"]], axis=0), TN)
        g_o[u["c"], u["p"]] = jnp.where(same, (eye + ge[:PAIR]) * u["wc"], 0.0).astype(BF16)
        e_o[u["c"], u["p"]] = jnp.where(same, ge[PAIR:] * u["wc"], 0.0)


def _rwkv_chunk(r, lw, k, v, kk, a, tri, chunks):
    n = r.shape[0]
    tc = chunks * CHUNK
    row = lambda i: (i, 0)
    tok = pl.BlockSpec((tc, RWKV_WIDTH), row)
    mat = pl.BlockSpec((chunks, RWKV_PAIRS, PAIR, PAIR), lambda i: (i, 0, 0, 0))
    tok_shape = lambda dtype: jax.ShapeDtypeStruct((n, RWKV_WIDTH), dtype)
    mat_shape = lambda dtype: jax.ShapeDtypeStruct((n // CHUNK, RWKV_PAIRS, PAIR, PAIR), dtype)
    return pl.pallas_call(
        functools.partial(_rwkv_chunk_kernel, chunks),
        grid=(n // tc,),
        in_specs=[tok] * 6 + [pl.BlockSpec((CHUNK, CHUNK), lambda i: (0, 0))],
        out_specs=[tok, tok, mat, mat],
        out_shape=[tok_shape(BF16), tok_shape(F32), mat_shape(BF16), mat_shape(F32)],
        compiler_params=_cparams("parallel"),
        name="rwkv_chunk",
    )(r, lw, k, v, kk, a, tri)


def _rwkv_scan_kernel(chunks, q1_ref, q2_ref, g_ref, e_ref, bonus_ref, gate_ref, hmean_ref,
                      lnw_ref, lnb_ref, o_ref, s_ref, y_ref):
    @pl.when(pl.program_id(1) == 0)
    def _():
        s_ref[...] = jnp.zeros_like(s_ref)

    for c in range(chunks):
        rows = pl.ds(c * CHUNK, CHUNK)
        for p in range(RWKV_PAIRS):
            ls = slice(p * PAIR, (p + 1) * PAIR)
            s = s_ref[p]
            s_hi, s_lo = _split2(s)
            y_ref[rows, ls] = _dot(q1_ref[rows, ls], s_hi, NT) + q2_ref[rows, ls]
            g = g_ref[c, p]
            s_ref[p] = _dot(s_hi, g) + (_dot(s_lo, g) + e_ref[c, p])

    y = y_ref[...]
    hmean = hmean_ref[...]
    mean = _dot_exact_rhs(y, hmean)
    yc = y - mean
    var = _dot_exact_rhs(yc * yc, hmean)
    yn = yc * lax.rsqrt(var + LN_X_EPS) * lnw_ref[...] + lnb_ref[...]
    o_ref[...] = ((yn + bonus_ref[...]) * gate_ref[...]).astype(o_ref.dtype)


def _rwkv_scan(q1, q2, g, e, bonus, gate, hmean, lnw, lnb, batch, seq_len, chunks):
    n = q1.shape[0]
    tc = chunks * CHUNK
    steps = seq_len // tc
    row = lambda b, i: (b * steps + i, 0)
    const = lambda b, i: (0, 0)
    tok = pl.BlockSpec((tc, RWKV_WIDTH), row)
    mat = pl.BlockSpec((chunks, RWKV_PAIRS, PAIR, PAIR), lambda b, i: (b * steps + i, 0, 0, 0))
    vec = pl.BlockSpec((1, RWKV_WIDTH), const)
    return pl.pallas_call(
        functools.partial(_rwkv_scan_kernel, chunks),
        grid=(batch, steps),
        in_specs=[tok, tok, mat, mat, tok, tok,
                  pl.BlockSpec((RWKV_WIDTH, RWKV_WIDTH), const), vec, vec],
        out_specs=tok,
        out_shape=jax.ShapeDtypeStruct((n, RWKV_WIDTH), BF16),
        scratch_shapes=[pltpu.VMEM((RWKV_PAIRS, PAIR, PAIR), F32),
                        pltpu.VMEM((tc, RWKV_WIDTH), F32)],
        compiler_params=_cparams("parallel", "arbitrary"),
        name="rwkv_scan",
    )(q1, q2, g, e, bonus, gate, hmean, lnw, lnb)


def _sb_prep_kernel(q_ref, k_ref, v_ref, qg_ref, kg_ref, hmean_ref, q_o, k_o, v_o):
    hmean = hmean_ref[...]
    q = q_ref[...]
    k = k_ref[...]
    qn = q * lax.rsqrt(_dot_exact_rhs(q * q, hmean) + NORM_EPS) * qg_ref[...] * (LOG2_E * HEAD_DIM ** -0.5)
    kn = k * lax.rsqrt(_dot_exact_rhs(k * k, hmean) + NORM_EPS) * kg_ref[...]
    lane_lo = lax.broadcasted_iota(jnp.int32, qn.shape, 1) < HEAD_DIM
    qm = jnp.concatenate([jnp.where(lane_lo, qn, 0.0), jnp.where(lane_lo, 0.0, qn)], axis=1)
    q_o[...] = qm.astype(BF16)
    k_o[...] = kn.astype(BF16)
    v_o[...] = v_ref[...].astype(BF16)


def _sb_prep(proj, qg, kg, hmean, tm):
    n = proj.shape[0]
    qblk = RWKV_COLS // PAIR
    kblk = (RWKV_COLS + SB_WIDTH) // PAIR
    vblk = (RWKV_COLS + 2 * SB_WIDTH) // PAIR
    assert qblk * PAIR == RWKV_COLS
    const = lambda i, p: (0, 0)
    q_shape = jax.ShapeDtypeStruct((n, 2 * SB_WIDTH), BF16)
    k_shape = jax.ShapeDtypeStruct((n, SB_WIDTH), BF16)
    q_spec = pl.BlockSpec((tm, 2 * PAIR), lambda i, p: (i, p))
    k_spec = pl.BlockSpec((tm, PAIR), lambda i, p: (i, p))
    return pl.pallas_call(
        _sb_prep_kernel,
        grid=(n // tm, SB_PAIRS),
        in_specs=[pl.BlockSpec((tm, PAIR), lambda i, p: (i, qblk + p)),
                  pl.BlockSpec((tm, PAIR), lambda i, p: (i, kblk + p)),
                  pl.BlockSpec((tm, PAIR), lambda i, p: (i, vblk + p)),
                  pl.BlockSpec((1, PAIR), const), pl.BlockSpec((1, PAIR), const),
                  pl.BlockSpec((PAIR, PAIR), const)],
        out_specs=[q_spec, k_spec, k_spec],
        out_shape=[q_shape, k_shape, k_shape],
        compiler_params=_cparams("parallel", "parallel"),
        name="sb_prep",
    )(proj, proj, proj, qg, kg, hmean)


def _sb_attn_kernel(tq, q_ref, k_ref, v_ref, upper_ref, o_ref):
    qi = pl.program_id(2)
    upper = upper_ref[...]
    ti = lax.broadcasted_iota(jnp.int32, (tq, tq), 0)
    si = lax.broadcasted_iota(jnp.int32, (tq, tq), 1)
    causal = si < ti

    def key_rows(j):
        return pl.ds(pl.multiple_of(j * tq, tq), tq)

    def scores(js):
        return [_dot(q_ref[:, h * PAIR:(h + 1) * PAIR], k_ref[key_rows(j), :], NT) for j in js for h in range(2)]

    def consume(js, carry, masked):
        tiles = [(h, key_rows(j)) for j in js for h in range(2)]
        zs = scores(js)
        sps16, log_betas = [], []
        for z in zs:
            sp = jnp.maximum(z, 0.0) + jnp.log2(1.0 + jnp.exp2(_neg_abs(z)))
            if masked:
                sp = jnp.where(causal, sp, 0.0)
            sps16.append(sp.astype(BF16))
            log_betas.append(z - sp)
        laters = [_dot(sp16, upper) for sp16 in sps16]
        carry = list(carry)
        for (h, ks), log_beta, sp16, later in zip(tiles, log_betas, sps16, laters):
            used, acc = carry[h]
            w = jnp.exp2(log_beta - (later + used))
            if masked:
                w = jnp.where(causal, w, 0.0)
            carry[h] = (used + (later[:, 0:1] + sp16[:, 0:1].astype(F32)), acc + _dot(w.astype(BF16), v_ref[ks, :]))
        return tuple(carry)

    zero = (jnp.zeros((tq, 1), F32), jnp.zeros((tq, PAIR), F32))
    carry = consume([qi], (zero, zero), True)

    def group(top, size):
        return [top - d for d in range(size)]

    carry = lax.fori_loop(0, qi // KEY_BLOCKS,
                          lambda i, c: consume(group(qi - 1 - KEY_BLOCKS * i, KEY_BLOCKS), c, False), carry)
    size = KEY_BLOCKS // 2
    while size >= 1:
        rem = qi % (2 * size)
        carry = lax.cond(rem >= size, lambda c, rem=rem, size=size: consume(group(rem - 1, size), c, False),
                         lambda c: c, carry)
        size //= 2
    lane_lo = lax.broadcasted_iota(jnp.int32, (tq, PAIR), 1) < HEAD_DIM
    o_ref[...] = jnp.where(lane_lo, carry[0][1], carry[1][1]).astype(o_ref.dtype)


def _sb_attn(qm, kn, vb, upper, batch, seq_len, tq):
    n = qm.shape[0]
    steps = seq_len // tq
    q_spec = pl.BlockSpec((tq, 2 * PAIR), lambda b, p, i: (b * steps + i, p))
    k_spec = pl.BlockSpec((seq_len, PAIR), lambda b, p, i: (b, p))
    return pl.pallas_call(
        functools.partial(_sb_attn_kernel, tq),
        grid=(batch, SB_PAIRS, steps),
        in_specs=[q_spec, k_spec, k_spec,
                  pl.BlockSpec((tq, tq), lambda b, p, i: (0, 0))],
        out_specs=pl.BlockSpec((tq, PAIR), lambda b, p, i: (b * steps + i, p)),
        out_shape=jax.ShapeDtypeStruct((n, SB_WIDTH), BF16),
        compiler_params=_cparams("parallel", "parallel", "arbitrary"),
        name="sb_attn",
    )(qm, kn, vb, upper)


def _pool_kernel(blocks_per_seq, seq_len, u_ref, halo_ref, w_ref, b_ref, s_ref, o_ref):
    u = u_ref[...]
    tm = u.shape[0]
    blk = pl.program_id(0) % blocks_per_seq
    halo = jnp.where(blk == 0, 0.0, halo_ref[...])
    ext = jnp.concatenate([halo, u], axis=0)
    pos = (blk * tm + lax.broadcasted_iota(jnp.int32, (tm, 1), 0)).astype(F32)
    group = lax.broadcasted_iota(jnp.int32, (tm, POOL_WIDTH), 1) // POOL_GROUP_DIM
    acc = ext
    span = 1
    pooled = jnp.zeros((tm, POOL_WIDTH), F32)
    for gi, win in enumerate(POOL_WINDOWS):
        while span < win:
            acc = acc + pltpu.roll(acc, span, axis=0)
            span *= 2
        count = jnp.minimum(pos + 1.0, float(win))
        pooled = jnp.where(group == gi, acc[POOL_HALO:] / count, pooled)
    pooled = pooled - u
    o_ref[...] = ((_dot(pooled.astype(BF16), w_ref[...]) + b_ref[...]) * s_ref[...]).astype(o_ref.dtype)


def _pool(proj, w_bd, bias, scale, seq_len, tm):
    n = proj.shape[0]
    ublk = (RWKV_COLS + SB_COLS) // POOL_WIDTH
    assert ublk * POOL_WIDTH == RWKV_COLS + SB_COLS
    const = lambda i: (0, 0)
    halo_idx = lambda i: (jnp.maximum(i * (tm // POOL_HALO) - 1, 0), ublk)
    return pl.pallas_call(
        functools.partial(_pool_kernel, seq_len // tm, seq_len),
        grid=(n // tm,),
        in_specs=[pl.BlockSpec((tm, POOL_WIDTH), lambda i: (i, ublk)),
                  pl.BlockSpec((POOL_HALO, POOL_WIDTH), halo_idx),
                  pl.BlockSpec((POOL_WIDTH, POOL_WIDTH), const),
                  pl.BlockSpec((1, POOL_WIDTH), const), pl.BlockSpec((1, POOL_WIDTH), const)],
        out_specs=pl.BlockSpec((tm, POOL_WIDTH), lambda i: (i, 0)),
        out_shape=jax.ShapeDtypeStruct((n, POOL_WIDTH), BF16),
        compiler_params=_cparams("parallel"),
        name="pool",
    )(proj, proj, w_bd, bias, scale)


def _mix_ffn_kernel(blocks_per_seq, tf, x_ref, xh_ref, yr_ref, yrh_ref, ys_ref, ysh_ref, yp_ref, yph_ref,
                    wr_ref, ws_ref, wp_ref, g_ref, wup_ref, cw_ref, cb_ref, wd_ref, o_ref, h_ref, act_ref):
    first = (pl.program_id(0) % blocks_per_seq) == 0
    g = g_ref[...]

    def mixed(x, yr, ys, yp):
        return x + (_dot(yr, wr_ref[...]) + (_dot(ys, ws_ref[...]) + _dot(yp, wp_ref[...])))

    x = mixed(x_ref[...], yr_ref[...], ys_ref[...], yp_ref[...])
    tail = slice(BF16_ROWS - SUBLANES, BF16_ROWS)
    x_prev = mixed(xh_ref[...], yrh_ref[tail, :], ysh_ref[tail, :], yph_ref[tail, :])
    h_ref[0:SUBLANES, :] = jnp.where(first, 0.0, _rms_rows(x_prev, g)).astype(BF16)
    h_ref[SUBLANES:, :] = _rms_rows(x, g).astype(BF16)
    h = h_ref[...]

    def conv(col):
        cols = slice(col, col + tf)
        up = _dot(h, wup_ref[:, cols])
        cw = cw_ref[:, cols]
        c = cb_ref[:, cols] + up[SUBLANES:] * cw[2:3]
        c = c + pltpu.roll(up, 1, axis=0)[SUBLANES:] * cw[1:2]
        return c + pltpu.roll(up, 2, axis=0)[SUBLANES:] * cw[0:1]

    for j in range(D_FF // tf):
        gate = conv(j * tf)
        val = conv(D_FF + j * tf)
        act_ref[:, j * tf:(j + 1) * tf] = (gate * _sigmoid(gate) * val).astype(BF16)
    o_ref[...] = x + _dot(act_ref[...], wd_ref[...])


def _mix_ffn(x2, yr, ys, yp, w_out_bf16, g, w_up_bf16, conv_w, conv_b, w_down_bf16, seq_len, tm, tf):
    n = x2.shape[0]
    row = lambda i: (i, 0)
    const = lambda i: (0, 0)
    halo = lambda rows: (lambda i: (jnp.maximum(i * (tm // rows) - 1, 0), 0))
    resident = lambda shape: pl.BlockSpec(shape, const, pipeline_mode=pl.Buffered(1))

    def with_halo(width, rows):
        return [pl.BlockSpec((tm, width), row), pl.BlockSpec((rows, width), halo(rows))]

    wr = w_out_bf16[:RWKV_WIDTH]
    ws = w_out_bf16[RWKV_WIDTH:RWKV_WIDTH + SB_WIDTH]
    wp = w_out_bf16[RWKV_WIDTH + SB_WIDTH:]
    return pl.pallas_call(
        functools.partial(_mix_ffn_kernel, seq_len // tm, tf),
        grid=(n // tm,),
        in_specs=(with_halo(D_MODEL, SUBLANES) + with_halo(RWKV_WIDTH, BF16_ROWS) + with_halo(SB_WIDTH, BF16_ROWS)
                  + with_halo(POOL_WIDTH, BF16_ROWS)
                  + [resident((RWKV_WIDTH, D_MODEL)), resident((SB_WIDTH, D_MODEL)), resident((POOL_WIDTH, D_MODEL)),
                     pl.BlockSpec((1, D_MODEL), const),
                     resident((D_MODEL, 2 * D_FF)),
                     pl.BlockSpec((3, 2 * D_FF), const),
                     pl.BlockSpec((1, 2 * D_FF), const),
                     resident((D_FF, D_MODEL))]),
        out_specs=pl.BlockSpec((tm, D_MODEL), row),
        out_shape=jax.ShapeDtypeStruct((n, D_MODEL), F32),
        scratch_shapes=[pltpu.VMEM((tm + SUBLANES, D_MODEL), BF16),
                        pltpu.VMEM((tm, D_FF), BF16)],
        compiler_params=_cparams("parallel"),
        name="mix_ffn",
    )(x2, x2, yr, yr, ys, ys, yp, yp, wr, ws, wp, g, w_up_bf16, conv_w, conv_b.reshape(1, 2 * D_FF), w_down_bf16)


def _head_indicator(width, value):
    idx = jnp.arange(width) // HEAD_DIM
    return jnp.where(idx[:, None] == idx[None, :], value, 0.0).astype(BF16)


def kernel(x, ln1_g, w_in, mu_shift, w0, w2, a0, a2, g2, k_k, k_a, r_k, lnx_w, lnx_b, v0, v1, v2, q_gain, k_gain, pool_w, pool_b, pool_scale, w_out, ln2_g, w_up, conv_w, conv_b, w_down):
    batch, seq_len, _ = x.shape
    depth = w_in.shape[0]
    n = batch * seq_len
    tm = min(512, seq_len)
    tq = min(256, seq_len)
    scan_chunks = min(8, seq_len // CHUNK)
    xform_chunks = min(4, seq_len // CHUNK)
    assert seq_len % tm == 0 and seq_len % (scan_chunks * CHUNK) == 0 and tm % POOL_HALO == 0

    hsum = _head_indicator(RWKV_WIDTH, 1.0)
    hmean = _head_indicator(RWKV_WIDTH, 1.0 / HEAD_DIM)
    tri = jnp.tril(jnp.ones((CHUNK, CHUNK), F32)).astype(BF16)
    upper = jnp.tril(jnp.ones((tq, tq), F32), -1).astype(BF16)
    row = lambda a: a.reshape(1, -1)

    x2 = x.reshape(n, D_MODEL)
    v_first = None
    for l in range(depth):
        proj = _inproj(x2, row(ln1_g[l]), w_in[l].astype(BF16), tm)

        params = (row(mu_shift[l]), row(w0[l]), w2[l], row(a0[l]), a2[l], g2[l],
                  row(k_k[l]), row(k_a[l]), row(r_k[l]), hsum)
        vmix = None if l == 0 else (v_first, row(v0[l - 1]), v1[l - 1], v2[l - 1])
        r, lw, k, v, kk, a, gate, bonus = _rwkv_prep(proj, seq_len, tm, params, vmix)
        if l == 0:
            v_first = v
        q1, q2, g_mat, e_mat = _rwkv_chunk(r, lw, k, v, kk, a, tri, xform_chunks)
        y_rwkv = _rwkv_scan(q1, q2, g_mat, e_mat, bonus, gate, hmean, row(lnx_w[l]), row(lnx_b[l]),
                            batch, seq_len, scan_chunks)

        sb_in = _sb_prep(proj, row(jnp.tile(q_gain[l], PAIR // HEAD_DIM)),
                         row(jnp.tile(k_gain[l], PAIR // HEAD_DIM)), hmean[:PAIR, :PAIR], tm)
        y_sb = _sb_attn(*sb_in, upper, batch, seq_len, tq)

        w_bd = jax.scipy.linalg.block_diag(*[pool_w[l, gi] for gi in range(len(POOL_WINDOWS))])
        y_pool = _pool(proj, w_bd.astype(BF16), row(pool_b[l]), row(pool_scale[l]), seq_len, tm)

        x2 = _mix_ffn(x2, y_rwkv, y_sb, y_pool, w_out[l].astype(BF16), row(ln2_g[l]), w_up[l].astype(BF16),
                      conv_w[l], conv_b[l], w_down[l].astype(BF16), seq_len, tm, 256)
    return x2.reshape(batch, seq_len, D_MODEL)
```

```python
import functools

import jax
import jax.numpy as jnp
from jax import lax
from jax.experimental import pallas as pl
from jax.experimental.pallas import tpu as pltpu

F32 = jnp.float32
BF16 = jnp.bfloat16

D_MODEL = 1024
HEAD_DIM = 64
RWKV_HEADS = 6
RWKV_WIDTH = RWKV_HEADS * HEAD_DIM
SB_HEADS = 6
SB_WIDTH = SB_HEADS * HEAD_DIM
POOL_WINDOWS = (2, 4, 8, 16)
POOL_WIDTH = D_MODEL - RWKV_WIDTH - SB_WIDTH
POOL_GROUP_DIM = POOL_WIDTH // len(POOL_WINDOWS)
DECAY_LORA = 64
AAA_LORA = 64
GATE_LORA = 128
MV_LORA = 32
RWKV_COLS = 3 * RWKV_WIDTH + DECAY_LORA + AAA_LORA + GATE_LORA
SB_COLS = 3 * SB_WIDTH
IN_COLS = RWKV_COLS + SB_COLS + POOL_WIDTH
D_FF = 2816
NORM_EPS = 1e-6
LN_X_EPS = 64e-5
L2_EPS = 1e-12
LOG2_E = 1.4426950408889634

SUBLANES = 8
LANES = 128
BF16_ROWS = 2 * SUBLANES
PAIR = 2 * HEAD_DIM
RWKV_PAIRS = RWKV_WIDTH // PAIR
SB_PAIRS = SB_WIDTH // PAIR
CHUNK = 64
POOL_HALO = 16
KEY_BLOCKS = 4
VMEM_LIMIT = 48 * 1024 * 1024

NN = (((1,), (0,)), ((), ()))
NT = (((1,), (1,)), ((), ()))
TN = (((0,), (0,)), ((), ()))


def _cparams(*sem):
    return pltpu.CompilerParams(dimension_semantics=sem, vmem_limit_bytes=VMEM_LIMIT)


def _dot(a, b, dims=NN):
    return lax.dot_general(a, b, dims, preferred_element_type=F32)


def _split2(x):
    hi = x.astype(BF16)
    lo = (x - hi.astype(F32)).astype(BF16)
    return hi, lo


def _neg_abs(x):
    return lax.bitcast_convert_type(lax.bitcast_convert_type(x, jnp.int32) | jnp.int32(-2 ** 31), F32)


def _split3(x):
    hi = x.astype(BF16)
    r1 = x - hi.astype(F32)
    mid = r1.astype(BF16)
    lo = (r1 - mid.astype(F32)).astype(BF16)
    return hi, mid, lo


def _dot_exact_rhs(a, b_bf16, dims=NN):
    h, l = _split2(a)
    return _dot(h, b_bf16, dims) + _dot(l, b_bf16, dims)


def _dot_exact_lhs(a_bf16, b, dims=NN):
    h, m, l = _split3(b)
    return _dot(a_bf16, h, dims) + (_dot(a_bf16, m, dims) + _dot(a_bf16, l, dims))


def _sigmoid(x):
    return 1.0 / (1.0 + jnp.exp(-x))


def _softplus(x):
    return jnp.maximum(x, 0.0) + jnp.log(1.0 + jnp.exp(-jnp.abs(x)))


def _rms_rows(x, g):
    return x * lax.rsqrt(jnp.mean(x * x, axis=-1, keepdims=True) + NORM_EPS) * g


def _front_kernel(blocks_per_seq, has_vmix, *refs):
    refs = list(refs)
    x_ref, xh_ref, ln_ref, win_ref = refs[:4]
    mu_ref, w0_ref, w2_ref, a0_ref, a2_ref, g2_ref, kk_ref, ka_ref, rk_ref, hsum_ref = refs[4:14]
    refs = refs[14:]
    if has_vmix:
        vf_ref, v0_ref, v1_ref, v2_ref = refs[:4]
        refs = refs[4:]
    qg_ref, kg_ref, hmean_ref, pw_ref, pb_ref, ps_ref = refs[:6]
    r_o, lw_o, k_o, v_o, kk_o, a_o, g_o, bonus_o, q_o, kn_o, vb_o, pool_o = refs[6:]

    tm = x_ref.shape[0]
    blk = pl.program_id(0) % blocks_per_seq
    first = blk == 0
    ln = ln_ref[...]
    w_in = win_ref[...]
    proj = _dot(_rms_rows(x_ref[...], ln).astype(BF16), w_in)
    proj_prev = jnp.where(first, 0.0, _dot(_rms_rows(xh_ref[...], ln).astype(BF16), w_in))

    p = proj[:, :RWKV_COLS]
    prev_last = proj_prev[POOL_HALO - 1:POOL_HALO, :RWKV_COLS]
    row = lax.broadcasted_iota(jnp.int32, (tm, 1), 0)
    shifted = jnp.where(row == 0, prev_last, pltpu.roll(p, 1, axis=0))
    p = p + (shifted - p) * mu_ref[...]

    c0, c1, c2 = RWKV_WIDTH, 2 * RWKV_WIDTH, 3 * RWKV_WIDTH
    r = p[:, 0:c0]
    k = p[:, c0:c1]
    v = p[:, c1:c2]
    xwa = p[:, c2:c2 + DECAY_LORA + AAA_LORA]
    xw = xwa[:, :DECAY_LORA]
    xa = xwa[:, DECAY_LORA:]
    xg = p[:, c2 + DECAY_LORA + AAA_LORA:]

    lora = lambda t, w_ref: _dot(t.astype(BF16), w_ref[...])
    w = -_softplus(-(w0_ref[...] + lora(jnp.tanh(xw), w2_ref))) - 0.5
    lw_o[...] = -jnp.exp(w)
    a = _sigmoid(a0_ref[...] + lora(xa, a2_ref))
    g_o[...] = lora(_sigmoid(xg), g2_ref)
    if has_vmix:
        mix = _sigmoid(v0_ref[...] + lora(lora(v, v1_ref), v2_ref))
        v = v + (vf_ref[...] - v) * mix
    hsum = hsum_ref[...]
    kk = k * kk_ref[...]
    kk = kk / jnp.maximum(jnp.sqrt(_dot_exact_rhs(kk * kk, hsum)), L2_EPS)
    k = k * (1.0 + (a - 1.0) * ka_ref[...])
    r_o[...] = r
    k_o[...] = k
    v_o[...] = v
    kk_o[...] = kk
    a_o[...] = a
    bonus_o[...] = _dot_exact_rhs(r * k * rk_ref[...], hsum) * v

    hmean = hmean_ref[...]
    lane_lo = lax.broadcasted_iota(jnp.int32, (tm, PAIR), 1) < HEAD_DIM
    for pair in range(SB_PAIRS):
        qc = RWKV_COLS + pair * PAIR
        q = proj[:, qc:qc + PAIR]
        kx = proj[:, qc + SB_WIDTH:qc + SB_WIDTH + PAIR]
        qn = q * lax.rsqrt(_dot_exact_rhs(q * q, hmean) + NORM_EPS) * qg_ref[...] * (LOG2_E * HEAD_DIM ** -0.5)
        kn = kx * lax.rsqrt(_dot_exact_rhs(kx * kx, hmean) + NORM_EPS) * kg_ref[...]
        q_o[:, 2 * pair * PAIR:(2 * pair + 1) * PAIR] = jnp.where(lane_lo, qn, 0.0).astype(BF16)
        q_o[:, (2 * pair + 1) * PAIR:(2 * pair + 2) * PAIR] = jnp.where(lane_lo, 0.0, qn).astype(BF16)
        kn_o[:, pair * PAIR:(pair + 1) * PAIR] = kn.astype(BF16)
    vb_o[...] = proj[:, RWKV_COLS + 2 * SB_WIDTH:RWKV_COLS + SB_COLS].astype(BF16)

    u = proj[:, RWKV_COLS + SB_COLS:]
    ext = jnp.concatenate([proj_prev[:, RWKV_COLS + SB_COLS:], u], axis=0)
    pos = (blk * tm + row).astype(F32)
    group = lax.broadcasted_iota(jnp.int32, (tm, POOL_WIDTH), 1) // POOL_GROUP_DIM
    acc = ext
    span = 1
    pooled = jnp.zeros((tm, POOL_WIDTH), F32)
    for gi, win in enumerate(POOL_WINDOWS):
        while span < win:
            acc = acc + pltpu.roll(acc, span, axis=0)
            span *= 2
        count = jnp.minimum(pos + 1.0, float(win))
        pooled = jnp.where(group == gi, acc[POOL_HALO:] / count, pooled)
    pooled = pooled - u
    pool_o[...] = ((_dot(pooled.astype(BF16), pw_ref[...]) + pb_ref[...]) * ps_ref[...]).astype(BF16)


def _front(x2, ln, w_in_bf16, rwkv_params, vmix, sb_params, pool_params, seq_len, tm):
    n = x2.shape[0]
    has_vmix = vmix is not None
    row = lambda i: (i, 0)
    const = lambda i: (0, 0)
    halo_idx = lambda i: (jnp.maximum(i * (tm // POOL_HALO) - 1, 0), 0)
    full = lambda a: pl.BlockSpec(a.shape, const)
    in_specs = [pl.BlockSpec((tm, D_MODEL), row),
                pl.BlockSpec((POOL_HALO, D_MODEL), halo_idx),
                full(ln),
                pl.BlockSpec((D_MODEL, IN_COLS), const, pipeline_mode=pl.Buffered(1))]
    in_specs += [full(a) for a in rwkv_params]
    args = [x2, x2, ln, w_in_bf16] + list(rwkv_params)
    if has_vmix:
        in_specs += [pl.BlockSpec((tm, RWKV_WIDTH), row)] + [full(a) for a in vmix[1:]]
        args += list(vmix)
    in_specs += [full(a) for a in sb_params + pool_params]
    args += list(sb_params + pool_params)
    tok = lambda width: pl.BlockSpec((tm, width), row)
    shape = lambda width, dtype: jax.ShapeDtypeStruct((n, width), dtype)
    return pl.pallas_call(
        functools.partial(_front_kernel, seq_len // tm, has_vmix),
        grid=(n // tm,),
        in_specs=in_specs,
        out_specs=[tok(RWKV_WIDTH)] * 8 + [tok(2 * SB_WIDTH), tok(SB_WIDTH), tok(SB_WIDTH), tok(POOL_WIDTH)],
        out_shape=[shape(RWKV_WIDTH, F32)] * 8 + [shape(2 * SB_WIDTH, BF16), shape(SB_WIDTH, BF16),
                                                  shape(SB_WIDTH, BF16), shape(POOL_WIDTH, BF16)],
        compiler_params=_cparams("parallel"),
        name="front",
    )(*args)


def _pair_rows(x, lane_lo):
    return jnp.concatenate([jnp.where(lane_lo, x, 0.0), jnp.where(lane_lo, 0.0, x)], axis=0)


def _rwkv_chunk_kernel(chunks, r_ref, lw_ref, k_ref, v_ref, kk_ref, a_ref, tri_ref,
                       q1_o, q2_o, g_o, e_o):
    two = 2 * CHUNK
    ri = lax.broadcasted_iota(jnp.int32, (two, two), 0)
    ci = lax.broadcasted_iota(jnp.int32, (two, two), 1)
    same = (ri // CHUNK) == (ci // CHUNK)
    strict = same & ((ci % CHUNK) < (ri % CHUNK))
    incl = same & ((ci % CHUNK) <= (ri % CHUNK))
    eye = (ri == ci).astype(F32)
    lane_lo = lax.broadcasted_iota(jnp.int32, (CHUNK, PAIR), 1) < HEAD_DIM
    tri = tri_ref[...]

    units = []
    for c in range(chunks):
        rows = pl.ds(c * CHUNK, CHUNK)
        lw = lw_ref[rows, :]
        cl = _dot_exact_lhs(tri, lw)
        e_pos = jnp.exp(cl)
        e_prev = jnp.exp(cl - lw)
        e_neg = jnp.exp(-cl)
        kk = kk_ref[rows, :]
        at = -kk * e_prev
        bt = kk * a_ref[rows, :] * e_neg
        rt = r_ref[rows, :] * e_pos
        kt = k_ref[rows, :] * e_neg
        v = v_ref[rows, :]
        wc = e_pos[CHUNK - 1:CHUNK, :]
        for p in range(RWKV_PAIRS):
            ls = slice(p * PAIR, (p + 1) * PAIR)
            units.append(dict(
                c=c, p=p, rows=rows, ls=ls, wc=wc[:, ls],
                xa=_pair_rows(at[:, ls], lane_lo).astype(BF16), xr=_pair_rows(rt[:, ls], lane_lo),
                vv=_pair_rows(v[:, ls], lane_lo).astype(BF16), v=v[:, ls].astype(BF16),
                bt=bt[:, ls].astype(BF16), kt=kt[:, ls].astype(BF16)))

    for u in units:
        sc = _dot(jnp.concatenate([u["xa"], u["xr"].astype(BF16)], axis=0),
                  jnp.concatenate([u["bt"], u["bt"], u["kt"], u["kt"]], axis=0), NT)
        a_ab = jnp.where(strict, sc[:two, :two], 0.0)
        u["a_ak"] = jnp.where(strict, sc[:two, two:], 0.0).astype(BF16)
        u["m"] = jnp.concatenate([jnp.where(incl, sc[two:, :two], 0.0),
                                  jnp.where(incl, sc[two:, two:], 0.0)], axis=1).astype(BF16)
        u["t"] = eye + a_ab
        u["pw"] = a_ab.astype(BF16)

    for _ in range(CHUNK.bit_length() - 2):
        for u in units:
            u["pw"] = _dot(u["pw"], u["pw"]).astype(BF16)
        for u in units:
            u["t"] = u["t"] + _dot(u["t"].astype(BF16), u["pw"])

    for u in units:
        u["z"] = _dot(u["a_ak"], u["vv"]).astype(BF16)
    for u in units:
        u["pm"] = _dot(u["t"].astype(BF16), jnp.concatenate([u["xa"], u["z"]], axis=1))
    for u in units:
        vv = u["vv"]
        qm = _dot(u["m"], jnp.concatenate([u["pm"].astype(BF16),
                                           jnp.concatenate([jnp.zeros_like(vv), vv], axis=1)], axis=0))
        q1m = u["xr"] + qm[:, :PAIR]
        q2m = qm[:, PAIR:]
        q1_o[u["rows"], u["ls"]] = (q1m[:CHUNK] + q1m[CHUNK:]).astype(BF16)
        q2_o[u["rows"], u["ls"]] = q2m[:CHUNK] + q2m[CHUNK:]
    for u in units:
        pm = u["pm"]
        p12 = (pm[:CHUNK] + pm[CHUNK:]).astype(BF16)
        lt = jnp.concatenate([p12, jnp.concatenate([jnp.zeros_like(u["bt"]), u["v"]], axis=1)], axis=0)
        ge = _dot(lt, jnp.concatenate([u["bt"], u["kt"]], axis=0), TN)
        g_o[u["c"], u["p"]] = jnp.where(same, (eye + ge[:PAIR]) * u["wc"], 0.0).astype(BF16)
        e_o[u["c"], u["p"]] = jnp.where(same, ge[PAIR:] * u["wc"], 0.0)


def _rwkv_chunk(r, lw, k, v, kk, a, tri, chunks):
    n = r.shape[0]
    tc = chunks * CHUNK
    row = lambda i: (i, 0)
    tok = pl.BlockSpec((tc, RWKV_WIDTH), row)
    mat = pl.BlockSpec((chunks, RWKV_PAIRS, PAIR, PAIR), lambda i: (i, 0, 0, 0))
    tok_shape = lambda dtype: jax.ShapeDtypeStruct((n, RWKV_WIDTH), dtype)
    mat_shape = lambda dtype: jax.ShapeDtypeStruct((n // CHUNK, RWKV_PAIRS, PAIR, PAIR), dtype)
    return pl.pallas_call(
        functools.partial(_rwkv_chunk_kernel, chunks),
        grid=(n // tc,),
        in_specs=[tok] * 6 + [pl.BlockSpec((CHUNK, CHUNK), lambda i: (0, 0))],
        out_specs=[tok, tok, mat, mat],
        out_shape=[tok_shape(BF16), tok_shape(F32), mat_shape(BF16), mat_shape(F32)],
        compiler_params=_cparams("parallel"),
        name="rwkv_chunk",
    )(r, lw, k, v, kk, a, tri)


def _rwkv_scan_kernel(chunks, q1_ref, q2_ref, g_ref, e_ref, bonus_ref, gate_ref, hmean_ref,
                      lnw_ref, lnb_ref, o_ref, s_ref, y_ref):
    @pl.when(pl.program_id(1) == 0)
    def _():
        s_ref[...] = jnp.zeros_like(s_ref)

    for c in range(chunks):
        rows = pl.ds(c * CHUNK, CHUNK)
        for p in range(RWKV_PAIRS):
            ls = slice(p * PAIR, (p + 1) * PAIR)
            s = s_ref[p]
            s_hi, s_lo = _split2(s)
            y_ref[rows, ls] = _dot(q1_ref[rows, ls], s_hi, NT) + q2_ref[rows, ls]
            g = g_ref[c, p]
            s_ref[p] = _dot(s_hi, g) + (_dot(s_lo, g) + e_ref[c, p])

    y = y_ref[...]
    hmean = hmean_ref[...]
    mean = _dot_exact_rhs(y, hmean)
    yc = y - mean
    var = _dot_exact_rhs(yc * yc, hmean)
    yn = yc * lax.rsqrt(var + LN_X_EPS) * lnw_ref[...] + lnb_ref[...]
    o_ref[...] = ((yn + bonus_ref[...]) * gate_ref[...]).astype(o_ref.dtype)


def _rwkv_scan(q1, q2, g, e, bonus, gate, hmean, lnw, lnb, batch, seq_len, chunks):
    n = q1.shape[0]
    tc = chunks * CHUNK
    steps = seq_len // tc
    row = lambda b, i: (b * steps + i, 0)
    const = lambda b, i: (0, 0)
    tok = pl.BlockSpec((tc, RWKV_WIDTH), row)
    mat = pl.BlockSpec((chunks, RWKV_PAIRS, PAIR, PAIR), lambda b, i: (b * steps + i, 0, 0, 0))
    vec = pl.BlockSpec((1, RWKV_WIDTH), const)
    return pl.pallas_call(
        functools.partial(_rwkv_scan_kernel, chunks),
        grid=(batch, steps),
        in_specs=[tok, tok, mat, mat, tok, tok,
                  pl.BlockSpec((RWKV_WIDTH, RWKV_WIDTH), const), vec, vec],
        out_specs=tok,
        out_shape=jax.ShapeDtypeStruct((n, RWKV_WIDTH), BF16),
        scratch_shapes=[pltpu.VMEM((RWKV_PAIRS, PAIR, PAIR), F32),
                        pltpu.VMEM((tc, RWKV_WIDTH), F32)],
        compiler_params=_cparams("parallel", "arbitrary"),
        name="rwkv_scan",
    )(q1, q2, g, e, bonus, gate, hmean, lnw, lnb)


def _sb_attn_kernel(tq, q_ref, k_ref, v_ref, upper_ref, o_ref):
    qi = pl.program_id(2)
    upper = upper_ref[...]
    ti = lax.broadcasted_iota(jnp.int32, (tq, tq), 0)
    si = lax.broadcasted_iota(jnp.int32, (tq, tq), 1)
    causal = si < ti

    def key_rows(j):
        return pl.ds(pl.multiple_of(j * tq, tq), tq)

    def scores(js):
        return [_dot(q_ref[:, h * PAIR:(h + 1) * PAIR], k_ref[key_rows(j), :], NT) for j in js for h in range(2)]

    def consume(top, size, carry, diagonal_first):
        tiles = [(h, key_rows(top - d), diagonal_first and d == 0) for d in range(size) for h in range(2)]
        zs = scores([top - d for d in range(size)])
        sps16, log_betas = [], []
        for (_, _, masked), z in zip(tiles, zs):
            sp = jnp.maximum(z, 0.0) + jnp.log2(1.0 + jnp.exp2(_neg_abs(z)))
            if masked:
                sp = jnp.where(causal, sp, 0.0)
            sps16.append(sp.astype(BF16))
            log_betas.append(z - sp)
        laters = [_dot(sp16, upper) for sp16 in sps16]
        carry = list(carry)
        for (h, ks, masked), log_beta, sp16, later in zip(tiles, log_betas, sps16, laters):
            used, acc = carry[h]
            w = jnp.exp2(log_beta - (later + used))
            if masked:
                w = jnp.where(causal, w, 0.0)
            carry[h] = (used + (later[:, 0:1] + sp16[:, 0:1].astype(F32)), acc + _dot(w.astype(BF16), v_ref[ks, :]))
        return tuple(carry)

    zero = (jnp.zeros((tq, 1), F32), jnp.zeros((tq, PAIR), F32))
    extra = qi % KEY_BLOCKS

    def first_group(size):
        if size == KEY_BLOCKS:
            return lambda c: consume(qi, size, c, True)
        return lambda c: lax.cond(extra == size - 1, lambda c2: consume(qi, size, c2, True), first_group(size + 1), c)

    carry = first_group(1)((zero, zero))
    carry = lax.fori_loop(0, qi // KEY_BLOCKS,
                          lambda i, c: consume(qi - extra - 1 - KEY_BLOCKS * i, KEY_BLOCKS, c, False), carry)
    lane_lo = lax.broadcasted_iota(jnp.int32, (tq, PAIR), 1) < HEAD_DIM
    o_ref[...] = jnp.where(lane_lo, carry[0][1], carry[1][1]).astype(o_ref.dtype)


def _sb_attn(qm, kn, vb, upper, batch, seq_len, tq):
    n = qm.shape[0]
    steps = seq_len // tq
    q_spec = pl.BlockSpec((tq, 2 * PAIR), lambda b, p, i: (b * steps + i, p))
    k_spec = pl.BlockSpec((seq_len, PAIR), lambda b, p, i: (b, p))
    return pl.pallas_call(
        functools.partial(_sb_attn_kernel, tq),
        grid=(batch, SB_PAIRS, steps),
        in_specs=[q_spec, k_spec, k_spec,
                  pl.BlockSpec((tq, tq), lambda b, p, i: (0, 0))],
        out_specs=pl.BlockSpec((tq, PAIR), lambda b, p, i: (b * steps + i, p)),
        out_shape=jax.ShapeDtypeStruct((n, SB_WIDTH), BF16),
        compiler_params=_cparams("parallel", "parallel", "arbitrary"),
        name="sb_attn",
    )(qm, kn, vb, upper)


def _mix_ffn_kernel(blocks_per_seq, tf, x_ref, xh_ref, yr_ref, yrh_ref, ys_ref, ysh_ref, yp_ref, yph_ref,
                    wr_ref, ws_ref, wp_ref, g_ref, wup_ref, cw_ref, cb_ref, wd_ref, o_ref, h_ref, act_ref):
    first = (pl.program_id(0) % blocks_per_seq) == 0
    g = g_ref[...]

    def mixed(x, yr, ys, yp):
        return x + (_dot(yr, wr_ref[...]) + (_dot(ys, ws_ref[...]) + _dot(yp, wp_ref[...])))

    x = mixed(x_ref[...], yr_ref[...], ys_ref[...], yp_ref[...])
    tail = slice(BF16_ROWS - SUBLANES, BF16_ROWS)
    x_prev = mixed(xh_ref[...], yrh_ref[tail, :], ysh_ref[tail, :], yph_ref[tail, :])
    h_ref[0:SUBLANES, :] = jnp.where(first, 0.0, _rms_rows(x_prev, g)).astype(BF16)
    h_ref[SUBLANES:, :] = _rms_rows(x, g).astype(BF16)
    h = h_ref[...]

    def conv(col):
        cols = slice(col, col + tf)
        up = _dot(h, wup_ref[:, cols])
        cw = cw_ref[:, cols]
        c = cb_ref[:, cols] + up[SUBLANES:] * cw[2:3]
        c = c + pltpu.roll(up, 1, axis=0)[SUBLANES:] * cw[1:2]
        return c + pltpu.roll(up, 2, axis=0)[SUBLANES:] * cw[0:1]

    for j in range(D_FF // tf):
        gate = conv(j * tf)
        val = conv(D_FF + j * tf)
        act_ref[:, j * tf:(j + 1) * tf] = (gate * _sigmoid(gate) * val).astype(BF16)
    o_ref[...] = x + _dot(act_ref[...], wd_ref[...])


def _mix_ffn(x2, yr, ys, yp, w_out_bf16, g, w_up_bf16, conv_w, conv_b, w_down_bf16, seq_len, tm, tf):
    n = x2.shape[0]
    row = lambda i: (i, 0)
    const = lambda i: (0, 0)
    halo = lambda rows: (lambda i: (jnp.maximum(i * (tm // rows) - 1, 0), 0))
    resident = lambda shape: pl.BlockSpec(shape, const, pipeline_mode=pl.Buffered(1))

    def with_halo(width, rows):
        return [pl.BlockSpec((tm, width), row), pl.BlockSpec((rows, width), halo(rows))]

    wr = w_out_bf16[:RWKV_WIDTH]
    ws = w_out_bf16[RWKV_WIDTH:RWKV_WIDTH + SB_WIDTH]
    wp = w_out_bf16[RWKV_WIDTH + SB_WIDTH:]
    return pl.pallas_call(
        functools.partial(_mix_ffn_kernel, seq_len // tm, tf),
        grid=(n // tm,),
        in_specs=(with_halo(D_MODEL, SUBLANES) + with_halo(RWKV_WIDTH, BF16_ROWS) + with_halo(SB_WIDTH, BF16_ROWS)
                  + with_halo(POOL_WIDTH, BF16_ROWS)
                  + [resident((RWKV_WIDTH, D_MODEL)), resident((SB_WIDTH, D_MODEL)), resident((POOL_WIDTH, D_MODEL)),
                     pl.BlockSpec((1, D_MODEL), const),
                     resident((D_MODEL, 2 * D_FF)),
                     pl.BlockSpec((3, 2 * D_FF), const),
                     pl.BlockSpec((1, 2 * D_FF), const),
                     resident((D_FF, D_MODEL))]),
        out_specs=pl.BlockSpec((tm, D_MODEL), row),
        out_shape=jax.ShapeDtypeStruct((n, D_MODEL), F32),
        scratch_shapes=[pltpu.VMEM((tm + SUBLANES, D_MODEL), BF16),
                        pltpu.VMEM((tm, D_FF), BF16)],
        compiler_params=_cparams("parallel"),
        name="mix_ffn",
    )(x2, x2, yr, yr, ys, ys, yp, yp, wr, ws, wp, g, w_up_bf16, conv_w, conv_b.reshape(1, 2 * D_FF), w_down_bf16)


def _head_indicator(width, value):
    idx = jnp.arange(width) // HEAD_DIM
    return jnp.where(idx[:, None] == idx[None, :], value, 0.0).astype(BF16)


def kernel(x, ln1_g, w_in, mu_shift, w0, w2, a0, a2, g2, k_k, k_a, r_k, lnx_w, lnx_b, v0, v1, v2, q_gain, k_gain, pool_w, pool_b, pool_scale, w_out, ln2_g, w_up, conv_w, conv_b, w_down):
    batch, seq_len, _ = x.shape
    depth = w_in.shape[0]
    n = batch * seq_len
    tm = min(512, seq_len)
    tq = min(256, seq_len)
    scan_chunks = min(8, seq_len // CHUNK)
    xform_chunks = min(4, seq_len // CHUNK)
    assert seq_len % tm == 0 and seq_len % (scan_chunks * CHUNK) == 0 and tm % POOL_HALO == 0

    hsum = _head_indicator(RWKV_WIDTH, 1.0)
    hmean = _head_indicator(RWKV_WIDTH, 1.0 / HEAD_DIM)
    tri = jnp.tril(jnp.ones((CHUNK, CHUNK), F32)).astype(BF16)
    upper = jnp.tril(jnp.ones((tq, tq), F32), -1).astype(BF16)
    row = lambda a: a.reshape(1, -1)

    x2 = x.reshape(n, D_MODEL)
    v_first = None
    for l in range(depth):
        rwkv_params = (row(mu_shift[l]), row(w0[l]), w2[l].astype(BF16), row(a0[l]), a2[l].astype(BF16),
                       g2[l].astype(BF16), row(k_k[l]), row(k_a[l]), row(r_k[l]), hsum)
        vmix = None if l == 0 else (v_first, row(v0[l - 1]), v1[l - 1].astype(BF16), v2[l - 1].astype(BF16))
        sb_params = (row(jnp.tile(q_gain[l], PAIR // HEAD_DIM)), row(jnp.tile(k_gain[l], PAIR // HEAD_DIM)),
                     hmean[:PAIR, :PAIR])
        w_bd = jax.scipy.linalg.block_diag(*[pool_w[l, gi] for gi in range(len(POOL_WINDOWS))])
        pool_params = (w_bd.astype(BF16), row(pool_b[l]), row(pool_scale[l]))
        (r, lw, k, v, kk, a, gate, bonus, qm, kn, vb, y_pool) = _front(
            x2, row(ln1_g[l]), w_in[l].astype(BF16), rwkv_params, vmix, sb_params, pool_params, seq_len, tm)
        if l == 0:
            v_first = v
        q1, q2, g_mat, e_mat = _rwkv_chunk(r, lw, k, v, kk, a, tri, xform_chunks)
        y_rwkv = _rwkv_scan(q1, q2, g_mat, e_mat, bonus, gate, hmean, row(lnx_w[l]), row(lnx_b[l]),
                            batch, seq_len, scan_chunks)
        y_sb = _sb_attn(qm, kn, vb, upper, batch, seq_len, tq)

        x2 = _mix_ffn(x2, y_rwkv, y_sb, y_pool, w_out[l].astype(BF16), row(ln2_g[l]), w_up[l].astype(BF16),
                      conv_w[l], conv_b[l], w_down[l].astype(BF16), seq_len, tm, 256)
    return x2.reshape(batch, seq_len, D_MODEL)
```

```python
import functools

import jax
import jax.numpy as jnp
from jax import lax
from jax.experimental import pallas as pl
from jax.experimental.pallas import tpu as pltpu

F32 = jnp.float32
BF16 = jnp.bfloat16

D_MODEL = 1024
HEAD_DIM = 64
RWKV_HEADS = 6
RWKV_WIDTH = RWKV_HEADS * HEAD_DIM
SB_HEADS = 6
SB_WIDTH = SB_HEADS * HEAD_DIM
POOL_WINDOWS = (2, 4, 8, 16)
POOL_WIDTH = D_MODEL - RWKV_WIDTH - SB_WIDTH
POOL_GROUP_DIM = POOL_WIDTH // len(POOL_WINDOWS)
DECAY_LORA = 64
AAA_LORA = 64
GATE_LORA = 128
MV_LORA = 32
RWKV_COLS = 3 * RWKV_WIDTH + DECAY_LORA + AAA_LORA + GATE_LORA
SB_COLS = 3 * SB_WIDTH
IN_COLS = RWKV_COLS + SB_COLS + POOL_WIDTH
D_FF = 2816
NORM_EPS = 1e-6
LN_X_EPS = 64e-5
L2_EPS = 1e-12
LOG2_E = 1.4426950408889634

SUBLANES = 8
LANES = 128
BF16_ROWS = 2 * SUBLANES
PAIR = 2 * HEAD_DIM
RWKV_PAIRS = RWKV_WIDTH // PAIR
SB_PAIRS = SB_WIDTH // PAIR
CHUNK = 64
POOL_HALO = 16
KEY_BLOCKS = 8
VMEM_LIMIT = 48 * 1024 * 1024

NN = (((1,), (0,)), ((), ()))
NT = (((1,), (1,)), ((), ()))
TN = (((0,), (0,)), ((), ()))


def _cparams(*sem):
    return pltpu.CompilerParams(dimension_semantics=sem, vmem_limit_bytes=VMEM_LIMIT)


def _dot(a, b, dims=NN):
    return lax.dot_general(a, b, dims, preferred_element_type=F32)


def _split2(x):
    hi = x.astype(BF16)
    lo = (x - hi.astype(F32)).astype(BF16)
    return hi, lo


def _neg_abs(x):
    return lax.bitcast_convert_type(lax.bitcast_convert_type(x, jnp.int32) | jnp.int32(-2 ** 31), F32)


def _split3(x):
    hi = x.astype(BF16)
    r1 = x - hi.astype(F32)
    mid = r1.astype(BF16)
    lo = (r1 - mid.astype(F32)).astype(BF16)
    return hi, mid, lo


def _dot_exact_rhs(a, b_bf16, dims=NN):
    h, l = _split2(a)
    return _dot(h, b_bf16, dims) + _dot(l, b_bf16, dims)


def _dot_exact_lhs(a_bf16, b, dims=NN):
    h, m, l = _split3(b)
    return _dot(a_bf16, h, dims) + (_dot(a_bf16, m, dims) + _dot(a_bf16, l, dims))


def _sigmoid(x):
    return 1.0 / (1.0 + jnp.exp(-x))


def _softplus(x):
    return jnp.maximum(x, 0.0) + jnp.log(1.0 + jnp.exp(-jnp.abs(x)))


def _rms_rows(x, g):
    return x * lax.rsqrt(jnp.mean(x * x, axis=-1, keepdims=True) + NORM_EPS) * g


def _front_kernel(blocks_per_seq, has_vmix, *refs):
    refs = list(refs)
    x_ref, xh_ref, ln_ref, win_ref = refs[:4]
    mu_ref, w0_ref, w2_ref, a0_ref, a2_ref, g2_ref, kk_ref, ka_ref, rk_ref, hsum_ref = refs[4:14]
    refs = refs[14:]
    if has_vmix:
        vf_ref, v0_ref, v1_ref, v2_ref = refs[:4]
        refs = refs[4:]
    qg_ref, kg_ref, hmean_ref, pw_ref, pb_ref, ps_ref = refs[:6]
    r_o, lw_o, k_o, v_o, kk_o, a_o, g_o, bonus_o, q_o, kn_o, vb_o, pool_o = refs[6:]

    tm = x_ref.shape[0]
    blk = pl.program_id(0) % blocks_per_seq
    first = blk == 0
    ln = ln_ref[...]
    w_in = win_ref[...]
    proj = _dot(_rms_rows(x_ref[...], ln).astype(BF16), w_in)
    proj_prev = jnp.where(first, 0.0, _dot(_rms_rows(xh_ref[...], ln).astype(BF16), w_in))

    p = proj[:, :RWKV_COLS]
    prev_last = proj_prev[POOL_HALO - 1:POOL_HALO, :RWKV_COLS]
    row = lax.broadcasted_iota(jnp.int32, (tm, 1), 0)
    shifted = jnp.where(row == 0, prev_last, pltpu.roll(p, 1, axis=0))
    p = p + (shifted - p) * mu_ref[...]

    c0, c1, c2 = RWKV_WIDTH, 2 * RWKV_WIDTH, 3 * RWKV_WIDTH
    r = p[:, 0:c0]
    k = p[:, c0:c1]
    v = p[:, c1:c2]
    xwa = p[:, c2:c2 + DECAY_LORA + AAA_LORA]
    xw = xwa[:, :DECAY_LORA]
    xa = xwa[:, DECAY_LORA:]
    xg = p[:, c2 + DECAY_LORA + AAA_LORA:]

    lora = lambda t, w_ref: _dot(t.astype(BF16), w_ref[...])
    w = -_softplus(-(w0_ref[...] + lora(jnp.tanh(xw), w2_ref))) - 0.5
    lw_o[...] = -jnp.exp(w)
    a = _sigmoid(a0_ref[...] + lora(xa, a2_ref))
    g_o[...] = lora(_sigmoid(xg), g2_ref)
    if has_vmix:
        mix = _sigmoid(v0_ref[...] + lora(lora(v, v1_ref), v2_ref))
        v = v + (vf_ref[...] - v) * mix
    hsum = hsum_ref[...]
    kk = k * kk_ref[...]
    kk = kk / jnp.maximum(jnp.sqrt(_dot_exact_rhs(kk * kk, hsum)), L2_EPS)
    k = k * (1.0 + (a - 1.0) * ka_ref[...])
    r_o[...] = r
    k_o[...] = k
    v_o[...] = v
    kk_o[...] = kk
    a_o[...] = a
    bonus_o[...] = _dot_exact_rhs(r * k * rk_ref[...], hsum) * v

    hmean = hmean_ref[...]
    lane_lo = lax.broadcasted_iota(jnp.int32, (tm, PAIR), 1) < HEAD_DIM
    for pair in range(SB_PAIRS):
        qc = RWKV_COLS + pair * PAIR
        q = proj[:, qc:qc + PAIR]
        kx = proj[:, qc + SB_WIDTH:qc + SB_WIDTH + PAIR]
        qn = q * lax.rsqrt(_dot_exact_rhs(q * q, hmean) + NORM_EPS) * qg_ref[...] * (LOG2_E * HEAD_DIM ** -0.5)
        kn = kx * lax.rsqrt(_dot_exact_rhs(kx * kx, hmean) + NORM_EPS) * kg_ref[...]
        q_o[:, 2 * pair * PAIR:(2 * pair + 1) * PAIR] = jnp.where(lane_lo, qn, 0.0).astype(BF16)
        q_o[:, (2 * pair + 1) * PAIR:(2 * pair + 2) * PAIR] = jnp.where(lane_lo, 0.0, qn).astype(BF16)
        kn_o[:, pair * PAIR:(pair + 1) * PAIR] = kn.astype(BF16)
    vb_o[...] = proj[:, RWKV_COLS + 2 * SB_WIDTH:RWKV_COLS + SB_COLS].astype(BF16)

    u = proj[:, RWKV_COLS + SB_COLS:]
    ext = jnp.concatenate([proj_prev[:, RWKV_COLS + SB_COLS:], u], axis=0)
    pos = (blk * tm + row).astype(F32)
    group = lax.broadcasted_iota(jnp.int32, (tm, POOL_WIDTH), 1) // POOL_GROUP_DIM
    acc = ext
    span = 1
    pooled = jnp.zeros((tm, POOL_WIDTH), F32)
    for gi, win in enumerate(POOL_WINDOWS):
        while span < win:
            acc = acc + pltpu.roll(acc, span, axis=0)
            span *= 2
        count = jnp.minimum(pos + 1.0, float(win))
        pooled = jnp.where(group == gi, acc[POOL_HALO:] / count, pooled)
    pooled = pooled - u
    pool_o[...] = ((_dot(pooled.astype(BF16), pw_ref[...]) + pb_ref[...]) * ps_ref[...]).astype(BF16)


def _front(x2, ln, w_in_bf16, rwkv_params, vmix, sb_params, pool_params, seq_len, tm):
    n = x2.shape[0]
    has_vmix = vmix is not None
    row = lambda i: (i, 0)
    const = lambda i: (0, 0)
    halo_idx = lambda i: (jnp.maximum(i * (tm // POOL_HALO) - 1, 0), 0)
    full = lambda a: pl.BlockSpec(a.shape, const)
    in_specs = [pl.BlockSpec((tm, D_MODEL), row),
                pl.BlockSpec((POOL_HALO, D_MODEL), halo_idx),
                full(ln),
                pl.BlockSpec((D_MODEL, IN_COLS), const, pipeline_mode=pl.Buffered(1))]
    in_specs += [full(a) for a in rwkv_params]
    args = [x2, x2, ln, w_in_bf16] + list(rwkv_params)
    if has_vmix:
        in_specs += [pl.BlockSpec((tm, RWKV_WIDTH), row)] + [full(a) for a in vmix[1:]]
        args += list(vmix)
    in_specs += [full(a) for a in sb_params + pool_params]
    args += list(sb_params + pool_params)
    tok = lambda width: pl.BlockSpec((tm, width), row)
    shape = lambda width, dtype: jax.ShapeDtypeStruct((n, width), dtype)
    return pl.pallas_call(
        functools.partial(_front_kernel, seq_len // tm, has_vmix),
        grid=(n // tm,),
        in_specs=in_specs,
        out_specs=[tok(RWKV_WIDTH)] * 8 + [tok(2 * SB_WIDTH), tok(SB_WIDTH), tok(SB_WIDTH), tok(POOL_WIDTH)],
        out_shape=[shape(RWKV_WIDTH, F32)] * 8 + [shape(2 * SB_WIDTH, BF16), shape(SB_WIDTH, BF16),
                                                  shape(SB_WIDTH, BF16), shape(POOL_WIDTH, BF16)],
        compiler_params=_cparams("parallel"),
        name="front",
    )(*args)


def _pair_rows(x, lane_lo):
    return jnp.concatenate([jnp.where(lane_lo, x, 0.0), jnp.where(lane_lo, 0.0, x)], axis=0)


def _rwkv_chunk_kernel(chunks, r_ref, lw_ref, k_ref, v_ref, kk_ref, a_ref, tri_ref,
                       q1_o, q2_o, g_o, e_o):
    two = 2 * CHUNK
    ri = lax.broadcasted_iota(jnp.int32, (two, two), 0)
    ci = lax.broadcasted_iota(jnp.int32, (two, two), 1)
    same = (ri // CHUNK) == (ci // CHUNK)
    strict = same & ((ci % CHUNK) < (ri % CHUNK))
    incl = same & ((ci % CHUNK) <= (ri % CHUNK))
    eye = (ri == ci).astype(F32)
    lane_lo = lax.broadcasted_iota(jnp.int32, (CHUNK, PAIR), 1) < HEAD_DIM
    tri = tri_ref[...]

    units = []
    for c in range(chunks):
        rows = pl.ds(c * CHUNK, CHUNK)
        lw = lw_ref[rows, :]
        cl = _dot_exact_lhs(tri, lw)
        e_pos = jnp.exp(cl)
        e_prev = jnp.exp(cl - lw)
        e_neg = jnp.exp(-cl)
        kk = kk_ref[rows, :]
        at = -kk * e_prev
        bt = kk * a_ref[rows, :] * e_neg
        rt = r_ref[rows, :] * e_pos
        kt = k_ref[rows, :] * e_neg
        v = v_ref[rows, :]
        wc = e_pos[CHUNK - 1:CHUNK, :]
        for p in range(RWKV_PAIRS):
            ls = slice(p * PAIR, (p + 1) * PAIR)
            units.append(dict(
                c=c, p=p, rows=rows, ls=ls, wc=wc[:, ls],
                xa=_pair_rows(at[:, ls], lane_lo).astype(BF16), xr=_pair_rows(rt[:, ls], lane_lo),
                vv=_pair_rows(v[:, ls], lane_lo).astype(BF16), v=v[:, ls].astype(BF16),
                bt=bt[:, ls].astype(BF16), kt=kt[:, ls].astype(BF16)))

    for u in units:
        sc = _dot(jnp.concatenate([u["xa"], u["xr"].astype(BF16)], axis=0),
                  jnp.concatenate([u["bt"], u["bt"], u["kt"], u["kt"]], axis=0), NT)
        a_ab = jnp.where(strict, sc[:two, :two], 0.0)
        u["a_ak"] = jnp.where(strict, sc[:two, two:], 0.0).astype(BF16)
        u["m"] = jnp.concatenate([jnp.where(incl, sc[two:, :two], 0.0),
                                  jnp.where(incl, sc[two:, two:], 0.0)], axis=1).astype(BF16)
        u["t"] = eye + a_ab
        u["pw"] = a_ab.astype(BF16)

    for _ in range(CHUNK.bit_length() - 2):
        for u in units:
            u["pw"] = _dot(u["pw"], u["pw"]).astype(BF16)
        for u in units:
            u["t"] = u["t"] + _dot(u["t"].astype(BF16), u["pw"])

    for u in units:
        u["z"] = _dot(u["a_ak"], u["vv"]).astype(BF16)
    for u in units:
        u["pm"] = _dot(u["t"].astype(BF16), jnp.concatenate([u["xa"], u["z"]], axis=1))
    for u in units:
        vv = u["vv"]
        qm = _dot(u["m"], jnp.concatenate([u["pm"].astype(BF16),
                                           jnp.concatenate([jnp.zeros_like(vv), vv], axis=1)], axis=0))
        q1m = u["xr"] + qm[:, :PAIR]
        q2m = qm[:, PAIR:]
        q1_o[u["rows"], u["ls"]] = (q1m[:CHUNK] + q1m[CHUNK:]).astype(BF16)
        q2_o[u["rows"], u["ls"]] = q2m[:CHUNK] + q2m[CHUNK:]
    for u in units:
        pm = u["pm"]
        p12 = (pm[:CHUNK] + pm[CHUNK:]).astype(BF16)
        lt = jnp.concatenate([p12, jnp.concatenate([jnp.zeros_like(u["bt"]), u["v"]], axis=1)], axis=0)
        ge = _dot(lt, jnp.concatenate([u["bt"], u["kt"]], axis=0), TN)
        g_o[u["c"], u["p"]] = jnp.where(same, (eye + ge[:PAIR]) * u["wc"], 0.0).astype(BF16)
        e_o[u["c"], u["p"]] = jnp.where(same, ge[PAIR:] * u["wc"], 0.0)


def _rwkv_chunk(r, lw, k, v, kk, a, tri, chunks):
    n = r.shape[0]
    tc = chunks * CHUNK
    row = lambda i: (i, 0)
    tok = pl.BlockSpec((tc, RWKV_WIDTH), row)
    mat = pl.BlockSpec((chunks, RWKV_PAIRS, PAIR, PAIR), lambda i: (i, 0, 0, 0))
    tok_shape = lambda dtype: jax.ShapeDtypeStruct((n, RWKV_WIDTH), dtype)
    mat_shape = lambda dtype: jax.ShapeDtypeStruct((n // CHUNK, RWKV_PAIRS, PAIR, PAIR), dtype)
    return pl.pallas_call(
        functools.partial(_rwkv_chunk_kernel, chunks),
        grid=(n // tc,),
        in_specs=[tok] * 6 + [pl.BlockSpec((CHUNK, CHUNK), lambda i: (0, 0))],
        out_specs=[tok, tok, mat, mat],
        out_shape=[tok_shape(BF16), tok_shape(F32), mat_shape(BF16), mat_shape(F32)],
        compiler_params=_cparams("parallel"),
        name="rwkv_chunk",
    )(r, lw, k, v, kk, a, tri)


def _rwkv_scan_kernel(chunks, q1_ref, q2_ref, g_ref, e_ref, bonus_ref, gate_ref, hmean_ref,
                      lnw_ref, lnb_ref, o_ref, s_ref, y_ref):
    @pl.when(pl.program_id(1) == 0)
    def _():
        s_ref[...] = jnp.zeros_like(s_ref)

    for c in range(chunks):
        rows = pl.ds(c * CHUNK, CHUNK)
        for p in range(RWKV_PAIRS):
            ls = slice(p * PAIR, (p + 1) * PAIR)
            s = s_ref[p]
            s_hi, s_lo = _split2(s)
            y_ref[rows, ls] = _dot(q1_ref[rows, ls], s_hi, NT) + q2_ref[rows, ls]
            g = g_ref[c, p]
            s_ref[p] = _dot(s_hi, g) + (_dot(s_lo, g) + e_ref[c, p])

    y = y_ref[...]
    hmean = hmean_ref[...]
    mean = _dot_exact_rhs(y, hmean)
    yc = y - mean
    var = _dot_exact_rhs(yc * yc, hmean)
    yn = yc * lax.rsqrt(var + LN_X_EPS) * lnw_ref[...] + lnb_ref[...]
    o_ref[...] = ((yn + bonus_ref[...]) * gate_ref[...]).astype(o_ref.dtype)


def _rwkv_scan(q1, q2, g, e, bonus, gate, hmean, lnw, lnb, batch, seq_len, chunks):
    n = q1.shape[0]
    tc = chunks * CHUNK
    steps = seq_len // tc
    row = lambda b, i: (b * steps + i, 0)
    const = lambda b, i: (0, 0)
    tok = pl.BlockSpec((tc, RWKV_WIDTH), row)
    mat = pl.BlockSpec((chunks, RWKV_PAIRS, PAIR, PAIR), lambda b, i: (b * steps + i, 0, 0, 0))
    vec = pl.BlockSpec((1, RWKV_WIDTH), const)
    return pl.pallas_call(
        functools.partial(_rwkv_scan_kernel, chunks),
        grid=(batch, steps),
        in_specs=[tok, tok, mat, mat, tok, tok,
                  pl.BlockSpec((RWKV_WIDTH, RWKV_WIDTH), const), vec, vec],
        out_specs=tok,
        out_shape=jax.ShapeDtypeStruct((n, RWKV_WIDTH), BF16),
        scratch_shapes=[pltpu.VMEM((RWKV_PAIRS, PAIR, PAIR), F32),
                        pltpu.VMEM((tc, RWKV_WIDTH), F32)],
        compiler_params=_cparams("parallel", "arbitrary"),
        name="rwkv_scan",
    )(q1, q2, g, e, bonus, gate, hmean, lnw, lnb)


def _sb_attn_kernel(tq, q_ref, k_ref, v_ref, upper_ref, o_ref):
    qi = pl.program_id(2)
    upper = upper_ref[...]
    ti = lax.broadcasted_iota(jnp.int32, (tq, tq), 0)
    si = lax.broadcasted_iota(jnp.int32, (tq, tq), 1)
    causal = si < ti

    def key_rows(j):
        return pl.ds(pl.multiple_of(j * tq, tq), tq)

    def consume(top, size, carry, diagonal_first):
        tiles = [(h, key_rows(top - d), diagonal_first and d == 0) for d in range(size) for h in range(2)]
        carry = list(carry)
        log_betas, sps16, laters = {}, {}, {}
        for step in range(len(tiles) + 2):
            t = step
            if t < len(tiles):
                h, ks, masked = tiles[t]
                z = _dot(q_ref[:, h * PAIR:(h + 1) * PAIR], k_ref[ks, :], NT)
                sp = jnp.maximum(z, 0.0) + jnp.log2(1.0 + jnp.exp2(_neg_abs(z)))
                if masked:
                    sp = jnp.where(causal, sp, 0.0)
                sps16[t] = sp.astype(BF16)
                log_betas[t] = z - sp
            t = step - 1
            if 0 <= t < len(tiles):
                laters[t] = _dot(sps16[t], upper)
            t = step - 2
            if 0 <= t < len(tiles):
                h, ks, masked = tiles[t]
                used, acc = carry[h]
                later = laters.pop(t)
                w = jnp.exp2(log_betas.pop(t) - (later + used))
                if masked:
                    w = jnp.where(causal, w, 0.0)
                sp_first = sps16.pop(t)[:, 0:1].astype(F32)
                carry[h] = (used + (later[:, 0:1] + sp_first), acc + _dot(w.astype(BF16), v_ref[ks, :]))
        return tuple(carry)

    zero = (jnp.zeros((tq, 1), F32), jnp.zeros((tq, PAIR), F32))
    extra = qi % KEY_BLOCKS

    def first_group(size):
        if size == KEY_BLOCKS:
            return lambda c: consume(qi, size, c, True)
        return lambda c: lax.cond(extra == size - 1, lambda c2: consume(qi, size, c2, True), first_group(size + 1), c)

    carry = first_group(1)((zero, zero))
    carry = lax.fori_loop(0, qi // KEY_BLOCKS,
                          lambda i, c: consume(qi - extra - 1 - KEY_BLOCKS * i, KEY_BLOCKS, c, False), carry)
    lane_lo = lax.broadcasted_iota(jnp.int32, (tq, PAIR), 1) < HEAD_DIM
    o_ref[...] = jnp.where(lane_lo, carry[0][1], carry[1][1]).astype(o_ref.dtype)


def _sb_attn(qm, kn, vb, upper, batch, seq_len, tq):
    n = qm.shape[0]
    steps = seq_len // tq
    q_spec = pl.BlockSpec((tq, 2 * PAIR), lambda b, p, i: (b * steps + i, p))
    k_spec = pl.BlockSpec((seq_len, PAIR), lambda b, p, i: (b, p))
    return pl.pallas_call(
        functools.partial(_sb_attn_kernel, tq),
        grid=(batch, SB_PAIRS, steps),
        in_specs=[q_spec, k_spec, k_spec,
                  pl.BlockSpec((tq, tq), lambda b, p, i: (0, 0))],
        out_specs=pl.BlockSpec((tq, PAIR), lambda b, p, i: (b * steps + i, p)),
        out_shape=jax.ShapeDtypeStruct((n, SB_WIDTH), BF16),
        compiler_params=_cparams("parallel", "parallel", "arbitrary"),
        name="sb_attn",
    )(qm, kn, vb, upper)


def _mix_ffn_kernel(blocks_per_seq, tf, x_ref, xh_ref, yr_ref, yrh_ref, ys_ref, ysh_ref, yp_ref, yph_ref,
                    wr_ref, ws_ref, wp_ref, g_ref, wup_ref, cw_ref, cb_ref, wd_ref, o_ref, h_ref, act_ref):
    first = (pl.program_id(0) % blocks_per_seq) == 0
    g = g_ref[...]

    def mixed(x, yr, ys, yp):
        return x + (_dot(yr, wr_ref[...]) + (_dot(ys, ws_ref[...]) + _dot(yp, wp_ref[...])))

    x = mixed(x_ref[...], yr_ref[...], ys_ref[...], yp_ref[...])
    tail = slice(BF16_ROWS - SUBLANES, BF16_ROWS)
    x_prev = mixed(xh_ref[...], yrh_ref[tail, :], ysh_ref[tail, :], yph_ref[tail, :])
    h_ref[0:SUBLANES, :] = jnp.where(first, 0.0, _rms_rows(x_prev, g)).astype(BF16)
    h_ref[SUBLANES:, :] = _rms_rows(x, g).astype(BF16)
    h = h_ref[...]

    def conv(col):
        cols = slice(col, col + tf)
        up = _dot(h, wup_ref[:, cols])
        cw = cw_ref[:, cols]
        c = cb_ref[:, cols] + up[SUBLANES:] * cw[2:3]
        c = c + pltpu.roll(up, 1, axis=0)[SUBLANES:] * cw[1:2]
        return c + pltpu.roll(up, 2, axis=0)[SUBLANES:] * cw[0:1]

    for j in range(D_FF // tf):
        gate = conv(j * tf)
        val = conv(D_FF + j * tf)
        act_ref[:, j * tf:(j + 1) * tf] = (gate * _sigmoid(gate) * val).astype(BF16)
    o_ref[...] = x + _dot(act_ref[...], wd_ref[...])


def _mix_ffn(x2, yr, ys, yp, w_out_bf16, g, w_up_bf16, conv_w, conv_b, w_down_bf16, seq_len, tm, tf):
    n = x2.shape[0]
    row = lambda i: (i, 0)
    const = lambda i: (0, 0)
    halo = lambda rows: (lambda i: (jnp.maximum(i * (tm // rows) - 1, 0), 0))
    resident = lambda shape: pl.BlockSpec(shape, const, pipeline_mode=pl.Buffered(1))

    def with_halo(width, rows):
        return [pl.BlockSpec((tm, width), row), pl.BlockSpec((rows, width), halo(rows))]

    wr = w_out_bf16[:RWKV_WIDTH]
    ws = w_out_bf16[RWKV_WIDTH:RWKV_WIDTH + SB_WIDTH]
    wp = w_out_bf16[RWKV_WIDTH + SB_WIDTH:]
    return pl.pallas_call(
        functools.partial(_mix_ffn_kernel, seq_len // tm, tf),
        grid=(n // tm,),
        in_specs=(with_halo(D_MODEL, SUBLANES) + with_halo(RWKV_WIDTH, BF16_ROWS) + with_halo(SB_WIDTH, BF16_ROWS)
                  + with_halo(POOL_WIDTH, BF16_ROWS)
                  + [resident((RWKV_WIDTH, D_MODEL)), resident((SB_WIDTH, D_MODEL)), resident((POOL_WIDTH, D_MODEL)),
                     pl.BlockSpec((1, D_MODEL), const),
                     resident((D_MODEL, 2 * D_FF)),
                     pl.BlockSpec((3, 2 * D_FF), const),
                     pl.BlockSpec((1, 2 * D_FF), const),
                     resident((D_FF, D_MODEL))]),
        out_specs=pl.BlockSpec((tm, D_MODEL), row),
        out_shape=jax.ShapeDtypeStruct((n, D_MODEL), F32),
        scratch_shapes=[pltpu.VMEM((tm + SUBLANES, D_MODEL), BF16),
                        pltpu.VMEM((tm, D_FF), BF16)],
        compiler_params=_cparams("parallel"),
        name="mix_ffn",
    )(x2, x2, yr, yr, ys, ys, yp, yp, wr, ws, wp, g, w_up_bf16, conv_w, conv_b.reshape(1, 2 * D_FF), w_down_bf16)


def _head_indicator(width, value):
    idx = jnp.arange(width) // HEAD_DIM
    return jnp.where(idx[:, None] == idx[None, :], value, 0.0).astype(BF16)


def kernel(x, ln1_g, w_in, mu_shift, w0, w2, a0, a2, g2, k_k, k_a, r_k, lnx_w, lnx_b, v0, v1, v2, q_gain, k_gain, pool_w, pool_b, pool_scale, w_out, ln2_g, w_up, conv_w, conv_b, w_down):
    batch, seq_len, _ = x.shape
    depth = w_in.shape[0]
    n = batch * seq_len
    tm = min(512, seq_len)
    tq = min(256, seq_len)
    scan_chunks = min(8, seq_len // CHUNK)
    xform_chunks = min(4, seq_len // CHUNK)
    assert seq_len % tm == 0 and seq_len % (scan_chunks * CHUNK) == 0 and tm % POOL_HALO == 0

    hsum = _head_indicator(RWKV_WIDTH, 1.0)
    hmean = _head_indicator(RWKV_WIDTH, 1.0 / HEAD_DIM)
    tri = jnp.tril(jnp.ones((CHUNK, CHUNK), F32)).astype(BF16)
    upper = jnp.tril(jnp.ones((tq, tq), F32), -1).astype(BF16)
    row = lambda a: a.reshape(1, -1)

    x2 = x.reshape(n, D_MODEL)
    v_first = None
    for l in range(depth):
        rwkv_params = (row(mu_shift[l]), row(w0[l]), w2[l].astype(BF16), row(a0[l]), a2[l].astype(BF16),
                       g2[l].astype(BF16), row(k_k[l]), row(k_a[l]), row(r_k[l]), hsum)
        vmix = None if l == 0 else (v_first, row(v0[l - 1]), v1[l - 1].astype(BF16), v2[l - 1].astype(BF16))
        sb_params = (row(jnp.tile(q_gain[l], PAIR // HEAD_DIM)), row(jnp.tile(k_gain[l], PAIR // HEAD_DIM)),
                     hmean[:PAIR, :PAIR])
        w_bd = jax.scipy.linalg.block_diag(*[pool_w[l, gi] for gi in range(len(POOL_WINDOWS))])
        pool_params = (w_bd.astype(BF16), row(pool_b[l]), row(pool_scale[l]))
        (r, lw, k, v, kk, a, gate, bonus, qm, kn, vb, y_pool) = _front(
            x2, row(ln1_g[l]), w_in[l].astype(BF16), rwkv_params, vmix, sb_params, pool_params, seq_len, tm)
        if l == 0:
            v_first = v
        q1, q2, g_mat, e_mat = _rwkv_chunk(r, lw, k, v, kk, a, tri, xform_chunks)
        y_rwkv = _rwkv_scan(q1, q2, g_mat, e_mat, bonus, gate, hmean, row(lnx_w[l]), row(lnx_b[l]),
                            batch, seq_len, scan_chunks)
        y_sb = _sb_attn(qm, kn, vb, upper, batch, seq_len, tq)

        x2 = _mix_ffn(x2, y_rwkv, y_sb, y_pool, w_out[l].astype(BF16), row(ln2_g[l]), w_up[l].astype(BF16),
                      conv_w[l], conv_b[l], w_down[l].astype(BF16), seq_len, tm, 256)
    return x2.reshape(batch, seq_len, D_MODEL)
```

```python
import functools

import jax
import jax.numpy as jnp
from jax import lax
from jax.experimental import pallas as pl
from jax.experimental.pallas import tpu as pltpu

F32 = jnp.float32
BF16 = jnp.bfloat16

D_MODEL = 1024
HEAD_DIM = 64
RWKV_HEADS = 6
RWKV_WIDTH = RWKV_HEADS * HEAD_DIM
SB_HEADS = 6
SB_WIDTH = SB_HEADS * HEAD_DIM
POOL_WINDOWS = (2, 4, 8, 16)
POOL_WIDTH = D_MODEL - RWKV_WIDTH - SB_WIDTH
POOL_GROUP_DIM = POOL_WIDTH // len(POOL_WINDOWS)
DECAY_LORA = 64
AAA_LORA = 64
GATE_LORA = 128
MV_LORA = 32
RWKV_COLS = 3 * RWKV_WIDTH + DECAY_LORA + AAA_LORA + GATE_LORA
SB_COLS = 3 * SB_WIDTH
IN_COLS = RWKV_COLS + SB_COLS + POOL_WIDTH
D_FF = 2816
NORM_EPS = 1e-6
LN_X_EPS = 64e-5
L2_EPS = 1e-12
LOG2_E = 1.4426950408889634

SUBLANES = 8
LANES = 128
BF16_ROWS = 2 * SUBLANES
PAIR = 2 * HEAD_DIM
RWKV_PAIRS = RWKV_WIDTH // PAIR
SB_PAIRS = SB_WIDTH // PAIR
CHUNK = 64
POOL_HALO = 16
KEY_BLOCKS = 8
VMEM_LIMIT = 48 * 1024 * 1024

NN = (((1,), (0,)), ((), ()))
NT = (((1,), (1,)), ((), ()))
TN = (((0,), (0,)), ((), ()))


def _cparams(*sem):
    return pltpu.CompilerParams(dimension_semantics=sem, vmem_limit_bytes=VMEM_LIMIT)


def _dot(a, b, dims=NN):
    return lax.dot_general(a, b, dims, preferred_element_type=F32)


def _split2(x):
    hi = x.astype(BF16)
    lo = (x - hi.astype(F32)).astype(BF16)
    return hi, lo


def _neg_abs(x):
    return lax.bitcast_convert_type(lax.bitcast_convert_type(x, jnp.int32) | jnp.int32(-2 ** 31), F32)


def _split3(x):
    hi = x.astype(BF16)
    r1 = x - hi.astype(F32)
    mid = r1.astype(BF16)
    lo = (r1 - mid.astype(F32)).astype(BF16)
    return hi, mid, lo


def _dot_exact_rhs(a, b_bf16, dims=NN):
    h, l = _split2(a)
    return _dot(h, b_bf16, dims) + _dot(l, b_bf16, dims)


def _dot_exact_lhs(a_bf16, b, dims=NN):
    h, m, l = _split3(b)
    return _dot(a_bf16, h, dims) + (_dot(a_bf16, m, dims) + _dot(a_bf16, l, dims))


def _sigmoid(x):
    return 1.0 / (1.0 + jnp.exp(-x))


def _softplus(x):
    return jnp.maximum(x, 0.0) + jnp.log(1.0 + jnp.exp(-jnp.abs(x)))


def _rms_rows(x, g):
    return x * lax.rsqrt(jnp.mean(x * x, axis=-1, keepdims=True) + NORM_EPS) * g


def _front_kernel(blocks_per_seq, has_vmix, *refs):
    refs = list(refs)
    x_ref, xh_ref, ln_ref, win_ref = refs[:4]
    mu_ref, w0_ref, w2_ref, a0_ref, a2_ref, g2_ref, kk_ref, ka_ref, rk_ref, hsum_ref = refs[4:14]
    refs = refs[14:]
    if has_vmix:
        vf_ref, v0_ref, v1_ref, v2_ref = refs[:4]
        refs = refs[4:]
    qg_ref, kg_ref, hmean_ref, pw_ref, pb_ref, ps_ref = refs[:6]
    r_o, lw_o, k_o, v_o, kk_o, a_o, g_o, bonus_o, q_o, kn_o, vb_o, pool_o = refs[6:]

    tm = x_ref.shape[0]
    blk = pl.program_id(0) % blocks_per_seq
    first = blk == 0
    ln = ln_ref[...]
    w_in = win_ref[...]
    proj = _dot(_rms_rows(x_ref[...], ln).astype(BF16), w_in)
    proj_prev = jnp.where(first, 0.0, _dot(_rms_rows(xh_ref[...], ln).astype(BF16), w_in))

    p = proj[:, :RWKV_COLS]
    prev_last = proj_prev[POOL_HALO - 1:POOL_HALO, :RWKV_COLS]
    row = lax.broadcasted_iota(jnp.int32, (tm, 1), 0)
    shifted = jnp.where(row == 0, prev_last, pltpu.roll(p, 1, axis=0))
    p = p + (shifted - p) * mu_ref[...]

    c0, c1, c2 = RWKV_WIDTH, 2 * RWKV_WIDTH, 3 * RWKV_WIDTH
    r = p[:, 0:c0]
    k = p[:, c0:c1]
    v = p[:, c1:c2]
    xwa = p[:, c2:c2 + DECAY_LORA + AAA_LORA]
    xw = xwa[:, :DECAY_LORA]
    xa = xwa[:, DECAY_LORA:]
    xg = p[:, c2 + DECAY_LORA + AAA_LORA:]

    lora = lambda t, w_ref: _dot(t.astype(BF16), w_ref[...])
    w = -_softplus(-(w0_ref[...] + lora(jnp.tanh(xw), w2_ref))) - 0.5
    lw_o[...] = -jnp.exp(w)
    a = _sigmoid(a0_ref[...] + lora(xa, a2_ref))
    g_o[...] = lora(_sigmoid(xg), g2_ref)
    if has_vmix:
        mix = _sigmoid(v0_ref[...] + lora(lora(v, v1_ref), v2_ref))
        v = v + (vf_ref[...] - v) * mix
    hsum = hsum_ref[...]
    kk = k * kk_ref[...]
    kk = kk / jnp.maximum(jnp.sqrt(_dot_exact_rhs(kk * kk, hsum)), L2_EPS)
    k = k * (1.0 + (a - 1.0) * ka_ref[...])
    r_o[...] = r
    k_o[...] = k
    v_o[...] = v
    kk_o[...] = kk
    a_o[...] = a
    bonus_o[...] = _dot_exact_rhs(r * k * rk_ref[...], hsum) * v

    hmean = hmean_ref[...]
    lane_lo = lax.broadcasted_iota(jnp.int32, (tm, PAIR), 1) < HEAD_DIM
    for pair in range(SB_PAIRS):
        qc = RWKV_COLS + pair * PAIR
        q = proj[:, qc:qc + PAIR]
        kx = proj[:, qc + SB_WIDTH:qc + SB_WIDTH + PAIR]
        qn = q * lax.rsqrt(_dot_exact_rhs(q * q, hmean) + NORM_EPS) * qg_ref[...] * (LOG2_E * HEAD_DIM ** -0.5)
        kn = kx * lax.rsqrt(_dot_exact_rhs(kx * kx, hmean) + NORM_EPS) * kg_ref[...]
        q_o[:, 2 * pair * PAIR:(2 * pair + 1) * PAIR] = jnp.where(lane_lo, qn, 0.0).astype(BF16)
        q_o[:, (2 * pair + 1) * PAIR:(2 * pair + 2) * PAIR] = jnp.where(lane_lo, 0.0, qn).astype(BF16)
        kn_o[:, pair * PAIR:(pair + 1) * PAIR] = kn.astype(BF16)
    vb_o[...] = proj[:, RWKV_COLS + 2 * SB_WIDTH:RWKV_COLS + SB_COLS].astype(BF16)

    u = proj[:, RWKV_COLS + SB_COLS:]
    ext = jnp.concatenate([proj_prev[:, RWKV_COLS + SB_COLS:], u], axis=0)
    pos = (blk * tm + row).astype(F32)
    group = lax.broadcasted_iota(jnp.int32, (tm, POOL_WIDTH), 1) // POOL_GROUP_DIM
    acc = ext
    span = 1
    pooled = jnp.zeros((tm, POOL_WIDTH), F32)
    for gi, win in enumerate(POOL_WINDOWS):
        while span < win:
            acc = acc + pltpu.roll(acc, span, axis=0)
            span *= 2
        count = jnp.minimum(pos + 1.0, float(win))
        pooled = jnp.where(group == gi, acc[POOL_HALO:] / count, pooled)
    pooled = pooled - u
    pool_o[...] = ((_dot(pooled.astype(BF16), pw_ref[...]) + pb_ref[...]) * ps_ref[...]).astype(BF16)


def _front(x2, ln, w_in_bf16, rwkv_params, vmix, sb_params, pool_params, seq_len, tm):
    n = x2.shape[0]
    has_vmix = vmix is not None
    row = lambda i: (i, 0)
    const = lambda i: (0, 0)
    halo_idx = lambda i: (jnp.maximum(i * (tm // POOL_HALO) - 1, 0), 0)
    full = lambda a: pl.BlockSpec(a.shape, const)
    in_specs = [pl.BlockSpec((tm, D_MODEL), row),
                pl.BlockSpec((POOL_HALO, D_MODEL), halo_idx),
                full(ln),
                pl.BlockSpec((D_MODEL, IN_COLS), const, pipeline_mode=pl.Buffered(1))]
    in_specs += [full(a) for a in rwkv_params]
    args = [x2, x2, ln, w_in_bf16] + list(rwkv_params)
    if has_vmix:
        in_specs += [pl.BlockSpec((tm, RWKV_WIDTH), row)] + [full(a) for a in vmix[1:]]
        args += list(vmix)
    in_specs += [full(a) for a in sb_params + pool_params]
    args += list(sb_params + pool_params)
    tok = lambda width: pl.BlockSpec((tm, width), row)
    shape = lambda width, dtype: jax.ShapeDtypeStruct((n, width), dtype)
    return pl.pallas_call(
        functools.partial(_front_kernel, seq_len // tm, has_vmix),
        grid=(n // tm,),
        in_specs=in_specs,
        out_specs=[tok(RWKV_WIDTH)] * 8 + [tok(2 * SB_WIDTH), tok(SB_WIDTH), tok(SB_WIDTH), tok(POOL_WIDTH)],
        out_shape=[shape(RWKV_WIDTH, F32)] * 8 + [shape(2 * SB_WIDTH, BF16), shape(SB_WIDTH, BF16),
                                                  shape(SB_WIDTH, BF16), shape(POOL_WIDTH, BF16)],
        compiler_params=_cparams("parallel"),
        name="front",
    )(*args)


def _pair_rows(x, lane_lo):
    return jnp.concatenate([jnp.where(lane_lo, x, 0.0), jnp.where(lane_lo, 0.0, x)], axis=0)


def _rwkv_chunk_kernel(chunks, r_ref, lw_ref, k_ref, v_ref, kk_ref, a_ref, tri_ref,
                       q1_o, q2_o, g_o, e_o):
    two = 2 * CHUNK
    ri = lax.broadcasted_iota(jnp.int32, (two, two), 0)
    ci = lax.broadcasted_iota(jnp.int32, (two, two), 1)
    same = (ri // CHUNK) == (ci // CHUNK)
    strict = same & ((ci % CHUNK) < (ri % CHUNK))
    incl = same & ((ci % CHUNK) <= (ri % CHUNK))
    eye = (ri == ci).astype(F32)
    lane_lo = lax.broadcasted_iota(jnp.int32, (CHUNK, PAIR), 1) < HEAD_DIM
    tri = tri_ref[...]

    units = []
    for c in range(chunks):
        rows = pl.ds(c * CHUNK, CHUNK)
        lw = lw_ref[rows, :]
        cl = _dot_exact_lhs(tri, lw)
        e_pos = jnp.exp(cl)
        e_prev = jnp.exp(cl - lw)
        e_neg = jnp.exp(-cl)
        kk = kk_ref[rows, :]
        at = -kk * e_prev
        bt = kk * a_ref[rows, :] * e_neg
        rt = r_ref[rows, :] * e_pos
        kt = k_ref[rows, :] * e_neg
        v = v_ref[rows, :]
        wc = e_pos[CHUNK - 1:CHUNK, :]
        for p in range(RWKV_PAIRS):
            ls = slice(p * PAIR, (p + 1) * PAIR)
            units.append(dict(
                c=c, p=p, rows=rows, ls=ls, wc=wc[:, ls],
                xa=_pair_rows(at[:, ls], lane_lo).astype(BF16), xr=_pair_rows(rt[:, ls], lane_lo),
                vv=_pair_rows(v[:, ls], lane_lo).astype(BF16), v=v[:, ls].astype(BF16),
                bt=bt[:, ls].astype(BF16), kt=kt[:, ls].astype(BF16)))

    for u in units:
        sc = _dot(jnp.concatenate([u["xa"], u["xr"].astype(BF16)], axis=0),
                  jnp.concatenate([u["bt"], u["bt"], u["kt"], u["kt"]], axis=0), NT)
        a_ab = jnp.where(strict, sc[:two, :two], 0.0)
        u["a_ak"] = jnp.where(strict, sc[:two, two:], 0.0).astype(BF16)
        u["m"] = jnp.concatenate([jnp.where(incl, sc[two:, :two], 0.0),
                                  jnp.where(incl, sc[two:, two:], 0.0)], axis=1).astype(BF16)
        u["t"] = eye + a_ab
        u["pw"] = a_ab.astype(BF16)

    for _ in range(CHUNK.bit_length() - 2):
        for u in units:
            u["pw"] = _dot(u["pw"], u["pw"]).astype(BF16)
        for u in units:
            u["t"] = u["t"] + _dot(u["t"].astype(BF16), u["pw"])

    for u in units:
        u["z"] = _dot(u["a_ak"], u["vv"]).astype(BF16)
    for u in units:
        u["pm"] = _dot(u["t"].astype(BF16), jnp.concatenate([u["xa"], u["z"]], axis=1))
    for u in units:
        vv = u["vv"]
        qm = _dot(u["m"], jnp.concatenate([u["pm"].astype(BF16),
                                           jnp.concatenate([jnp.zeros_like(vv), vv], axis=1)], axis=0))
        q1m = u["xr"] + qm[:, :PAIR]
        q2m = qm[:, PAIR:]
        q1_o[u["rows"], u["ls"]] = (q1m[:CHUNK] + q1m[CHUNK:]).astype(BF16)
        q2_o[u["rows"], u["ls"]] = q2m[:CHUNK] + q2m[CHUNK:]
    for u in units:
        pm = u["pm"]
        p12 = (pm[:CHUNK] + pm[CHUNK:]).astype(BF16)
        lt = jnp.concatenate([p12, jnp.concatenate([jnp.zeros_like(u["bt"]), u["v"]], axis=1)], axis=0)
        ge = _dot(lt, jnp.concatenate([u["bt"], u["kt"]], axis=0), TN)
        g_o[u["c"], u["p"]] = jnp.where(same, (eye + ge[:PAIR]) * u["wc"], 0.0).astype(BF16)
        e_o[u["c"], u["p"]] = jnp.where(same, ge[PAIR:] * u["wc"], 0.0)


def _rwkv_chunk(r, lw, k, v, kk, a, tri, chunks):
    n = r.shape[0]
    tc = chunks * CHUNK
    row = lambda i: (i, 0)
    tok = pl.BlockSpec((tc, RWKV_WIDTH), row)
    mat = pl.BlockSpec((chunks, RWKV_PAIRS, PAIR, PAIR), lambda i: (i, 0, 0, 0))
    tok_shape = lambda dtype: jax.ShapeDtypeStruct((n, RWKV_WIDTH), dtype)
    mat_shape = lambda dtype: jax.ShapeDtypeStruct((n // CHUNK, RWKV_PAIRS, PAIR, PAIR), dtype)
    return pl.pallas_call(
        functools.partial(_rwkv_chunk_kernel, chunks),
        grid=(n // tc,),
        in_specs=[tok] * 6 + [pl.BlockSpec((CHUNK, CHUNK), lambda i: (0, 0))],
        out_specs=[tok, tok, mat, mat],
        out_shape=[tok_shape(BF16), tok_shape(F32), mat_shape(BF16), mat_shape(F32)],
        compiler_params=_cparams("parallel"),
        name="rwkv_chunk",
    )(r, lw, k, v, kk, a, tri)


def _rwkv_scan_kernel(chunks, q1_ref, q2_ref, g_ref, e_ref, bonus_ref, gate_ref, hmean_ref,
                      lnw_ref, lnb_ref, o_ref, s_ref, y_ref):
    @pl.when(pl.program_id(1) == 0)
    def _():
        s_ref[...] = jnp.zeros_like(s_ref)

    for c in range(chunks):
        rows = pl.ds(c * CHUNK, CHUNK)
        for p in range(RWKV_PAIRS):
            ls = slice(p * PAIR, (p + 1) * PAIR)
            s = s_ref[p]
            s_hi, s_lo = _split2(s)
            y_ref[rows, ls] = _dot(q1_ref[rows, ls], s_hi, NT) + q2_ref[rows, ls]
            g = g_ref[c, p]
            s_ref[p] = _dot(s_hi, g) + (_dot(s_lo, g) + e_ref[c, p])

    y = y_ref[...]
    hmean = hmean_ref[...]
    mean = _dot_exact_rhs(y, hmean)
    yc = y - mean
    var = _dot_exact_rhs(yc * yc, hmean)
    yn = yc * lax.rsqrt(var + LN_X_EPS) * lnw_ref[...] + lnb_ref[...]
    o_ref[...] = ((yn + bonus_ref[...]) * gate_ref[...]).astype(o_ref.dtype)


def _rwkv_scan(q1, q2, g, e, bonus, gate, hmean, lnw, lnb, batch, seq_len, chunks):
    n = q1.shape[0]
    tc = chunks * CHUNK
    steps = seq_len // tc
    row = lambda b, i: (b * steps + i, 0)
    const = lambda b, i: (0, 0)
    tok = pl.BlockSpec((tc, RWKV_WIDTH), row)
    mat = pl.BlockSpec((chunks, RWKV_PAIRS, PAIR, PAIR), lambda b, i: (b * steps + i, 0, 0, 0))
    vec = pl.BlockSpec((1, RWKV_WIDTH), const)
    return pl.pallas_call(
        functools.partial(_rwkv_scan_kernel, chunks),
        grid=(batch, steps),
        in_specs=[tok, tok, mat, mat, tok, tok,
                  pl.BlockSpec((RWKV_WIDTH, RWKV_WIDTH), const), vec, vec],
        out_specs=tok,
        out_shape=jax.ShapeDtypeStruct((n, RWKV_WIDTH), BF16),
        scratch_shapes=[pltpu.VMEM((RWKV_PAIRS, PAIR, PAIR), F32),
                        pltpu.VMEM((tc, RWKV_WIDTH), F32)],
        compiler_params=_cparams("parallel", "arbitrary"),
        name="rwkv_scan",
    )(q1, q2, g, e, bonus, gate, hmean, lnw, lnb)


def _sb_attn_kernel(tq, tk, q_ref, k_ref, v_ref, upper_ref, o_ref):
    qi = pl.program_id(2)
    sub = tq // tk
    neg_upper = upper_ref[...]
    ti = lax.broadcasted_iota(jnp.int32, (tq, tk), 0)
    si = lax.broadcasted_iota(jnp.int32, (tq, tk), 1)
    causal_of = [si + (sub - 1 - d) * tk < ti for d in range(sub)]

    def key_rows(j):
        return pl.ds(pl.multiple_of(j * tk, tk), tk)

    def consume(top, size, carry, diagonal_first):
        tiles = [(h, key_rows(top - d), causal_of[d] if diagonal_first and d < sub else None)
                 for d in range(size) for h in range(2)]
        carry = list(carry)
        log_betas, sps16, totals, logits = {}, {}, {}, {}
        for step in range(len(tiles) + 2):
            t = step
            if t < len(tiles):
                h, ks, causal = tiles[t]
                z = _dot(q_ref[:, h * PAIR:(h + 1) * PAIR], k_ref[ks, :], NT)
                sp = jnp.maximum(z, 0.0) + jnp.log(1.0 + jnp.exp2(_neg_abs(z))) * LOG2_E
                if causal is not None:
                    sp = jnp.where(causal, sp, 0.0)
                sps16[t] = sp.astype(BF16)
                log_betas[t] = z - sp
                totals[t] = z[:, 0:1]
            t = step - 1
            if 0 <= t < len(tiles):
                logits[t] = log_betas.pop(t) + _dot(sps16.pop(t), neg_upper)
                totals[t] = totals[t] - logits[t][:, 0:1]
            t = step - 2
            if 0 <= t < len(tiles):
                h, ks, causal = tiles[t]
                used, acc = carry[h]
                w = jnp.exp2(logits.pop(t) - used)
                if causal is not None:
                    w = jnp.where(causal, w, 0.0)
                carry[h] = (used + totals.pop(t), acc + _dot(w.astype(BF16), v_ref[ks, :]))
        return tuple(carry)

    zero = (jnp.zeros((tq, 1), F32), jnp.zeros((tq, PAIR), F32))
    below = qi * sub
    extra = below % KEY_BLOCKS
    top = below + sub - 1

    def first_group(e):
        if e + sub >= KEY_BLOCKS:
            return lambda c: consume(top, sub + e, c, True)
        return lambda c: lax.cond(extra == e, lambda c2: consume(top, sub + e, c2, True), first_group(e + sub), c)

    carry = first_group(0)((zero, zero))
    carry = lax.fori_loop(0, below // KEY_BLOCKS,
                          lambda i, c: consume(below - extra - 1 - KEY_BLOCKS * i, KEY_BLOCKS, c, False), carry)
    lane_lo = lax.broadcasted_iota(jnp.int32, (tq, PAIR), 1) < HEAD_DIM
    o_ref[...] = jnp.where(lane_lo, carry[0][1], carry[1][1]).astype(o_ref.dtype)


def _sb_attn(qm, kn, vb, upper, batch, seq_len, tq, tk):
    n = qm.shape[0]
    steps = seq_len // tq
    assert tq % tk == 0 and KEY_BLOCKS % (tq // tk) == 0
    q_spec = pl.BlockSpec((tq, 2 * PAIR), lambda b, p, i: (b * steps + i, p))
    k_spec = pl.BlockSpec((seq_len, PAIR), lambda b, p, i: (b, p))
    return pl.pallas_call(
        functools.partial(_sb_attn_kernel, tq, tk),
        grid=(batch, SB_PAIRS, steps),
        in_specs=[q_spec, k_spec, k_spec,
                  pl.BlockSpec((tk, tk), lambda b, p, i: (0, 0))],
        out_specs=pl.BlockSpec((tq, PAIR), lambda b, p, i: (b * steps + i, p)),
        out_shape=jax.ShapeDtypeStruct((n, SB_WIDTH), BF16),
        compiler_params=_cparams("parallel", "parallel", "arbitrary"),
        name="sb_attn",
    )(qm, kn, vb, upper)


def _mix_ffn_kernel(blocks_per_seq, tf, x_ref, xh_ref, yr_ref, yrh_ref, ys_ref, ysh_ref, yp_ref, yph_ref,
                    wr_ref, ws_ref, wp_ref, g_ref, wup_ref, cw_ref, cb_ref, wd_ref, o_ref, h_ref, act_ref):
    first = (pl.program_id(0) % blocks_per_seq) == 0
    g = g_ref[...]

    def mixed(x, yr, ys, yp):
        return x + (_dot(yr, wr_ref[...]) + (_dot(ys, ws_ref[...]) + _dot(yp, wp_ref[...])))

    x = mixed(x_ref[...], yr_ref[...], ys_ref[...], yp_ref[...])
    tail = slice(BF16_ROWS - SUBLANES, BF16_ROWS)
    x_prev = mixed(xh_ref[...], yrh_ref[tail, :], ysh_ref[tail, :], yph_ref[tail, :])
    h_ref[0:SUBLANES, :] = jnp.where(first, 0.0, _rms_rows(x_prev, g)).astype(BF16)
    h_ref[SUBLANES:, :] = _rms_rows(x, g).astype(BF16)
    h = h_ref[...]

    def conv(col):
        cols = slice(col, col + tf)
        up = _dot(h, wup_ref[:, cols])
        cw = cw_ref[:, cols]
        c = cb_ref[:, cols] + up[SUBLANES:] * cw[2:3]
        c = c + pltpu.roll(up, 1, axis=0)[SUBLANES:] * cw[1:2]
        return c + pltpu.roll(up, 2, axis=0)[SUBLANES:] * cw[0:1]

    for j in range(D_FF // tf):
        gate = conv(j * tf)
        val = conv(D_FF + j * tf)
        act_ref[:, j * tf:(j + 1) * tf] = (gate * _sigmoid(gate) * val).astype(BF16)
    o_ref[...] = x + _dot(act_ref[...], wd_ref[...])


def _mix_ffn(x2, yr, ys, yp, w_out_bf16, g, w_up_bf16, conv_w, conv_b, w_down_bf16, seq_len, tm, tf):
    n = x2.shape[0]
    row = lambda i: (i, 0)
    const = lambda i: (0, 0)
    halo = lambda rows: (lambda i: (jnp.maximum(i * (tm // rows) - 1, 0), 0))
    resident = lambda shape: pl.BlockSpec(shape, const, pipeline_mode=pl.Buffered(1))

    def with_halo(width, rows):
        return [pl.BlockSpec((tm, width), row), pl.BlockSpec((rows, width), halo(rows))]

    wr = w_out_bf16[:RWKV_WIDTH]
    ws = w_out_bf16[RWKV_WIDTH:RWKV_WIDTH + SB_WIDTH]
    wp = w_out_bf16[RWKV_WIDTH + SB_WIDTH:]
    return pl.pallas_call(
        functools.partial(_mix_ffn_kernel, seq_len // tm, tf),
        grid=(n // tm,),
        in_specs=(with_halo(D_MODEL, SUBLANES) + with_halo(RWKV_WIDTH, BF16_ROWS) + with_halo(SB_WIDTH, BF16_ROWS)
                  + with_halo(POOL_WIDTH, BF16_ROWS)
                  + [resident((RWKV_WIDTH, D_MODEL)), resident((SB_WIDTH, D_MODEL)), resident((POOL_WIDTH, D_MODEL)),
                     pl.BlockSpec((1, D_MODEL), const),
                     resident((D_MODEL, 2 * D_FF)),
                     pl.BlockSpec((3, 2 * D_FF), const),
                     pl.BlockSpec((1, 2 * D_FF), const),
                     resident((D_FF, D_MODEL))]),
        out_specs=pl.BlockSpec((tm, D_MODEL), row),
        out_shape=jax.ShapeDtypeStruct((n, D_MODEL), F32),
        scratch_shapes=[pltpu.VMEM((tm + SUBLANES, D_MODEL), BF16),
                        pltpu.VMEM((tm, D_FF), BF16)],
        compiler_params=_cparams("parallel"),
        name="mix_ffn",
    )(x2, x2, yr, yr, ys, ys, yp, yp, wr, ws, wp, g, w_up_bf16, conv_w, conv_b.reshape(1, 2 * D_FF), w_down_bf16)


def _head_indicator(width, value):
    idx = jnp.arange(width) // HEAD_DIM
    return jnp.where(idx[:, None] == idx[None, :], value, 0.0).astype(BF16)


def kernel(x, ln1_g, w_in, mu_shift, w0, w2, a0, a2, g2, k_k, k_a, r_k, lnx_w, lnx_b, v0, v1, v2, q_gain, k_gain, pool_w, pool_b, pool_scale, w_out, ln2_g, w_up, conv_w, conv_b, w_down):
    batch, seq_len, _ = x.shape
    depth = w_in.shape[0]
    n = batch * seq_len
    tm = min(512, seq_len)
    tq = min(512, seq_len)
    tk = min(256, seq_len)
    scan_chunks = min(8, seq_len // CHUNK)
    xform_chunks = min(4, seq_len // CHUNK)
    assert seq_len % tm == 0 and seq_len % (scan_chunks * CHUNK) == 0 and tm % POOL_HALO == 0

    hsum = _head_indicator(RWKV_WIDTH, 1.0)
    hmean = _head_indicator(RWKV_WIDTH, 1.0 / HEAD_DIM)
    tri = jnp.tril(jnp.ones((CHUNK, CHUNK), F32)).astype(BF16)
    upper = (-jnp.tril(jnp.ones((tk, tk), F32), -1)).astype(BF16)
    row = lambda a: a.reshape(1, -1)

    x2 = x.reshape(n, D_MODEL)
    v_first = None
    for l in range(depth):
        rwkv_params = (row(mu_shift[l]), row(w0[l]), w2[l].astype(BF16), row(a0[l]), a2[l].astype(BF16),
                       g2[l].astype(BF16), row(k_k[l]), row(k_a[l]), row(r_k[l]), hsum)
        vmix = None if l == 0 else (v_first, row(v0[l - 1]), v1[l - 1].astype(BF16), v2[l - 1].astype(BF16))
        sb_params = (row(jnp.tile(q_gain[l], PAIR // HEAD_DIM)), row(jnp.tile(k_gain[l], PAIR // HEAD_DIM)),
                     hmean[:PAIR, :PAIR])
        w_bd = jax.scipy.linalg.block_diag(*[pool_w[l, gi] for gi in range(len(POOL_WINDOWS))])
        pool_params = (w_bd.astype(BF16), row(pool_b[l]), row(pool_scale[l]))
        (r, lw, k, v, kk, a, gate, bonus, qm, kn, vb, y_pool) = _front(
            x2, row(ln1_g[l]), w_in[l].astype(BF16), rwkv_params, vmix, sb_params, pool_params, seq_len, tm)
        if l == 0:
            v_first = v
        q1, q2, g_mat, e_mat = _rwkv_chunk(r, lw, k, v, kk, a, tri, xform_chunks)
        y_rwkv = _rwkv_scan(q1, q2, g_mat, e_mat, bonus, gate, hmean, row(lnx_w[l]), row(lnx_b[l]),
                            batch, seq_len, scan_chunks)
        y_sb = _sb_attn(qm, kn, vb, upper, batch, seq_len, tq, tk)

        x2 = _mix_ffn(x2, y_rwkv, y_sb, y_pool, w_out[l].astype(BF16), row(ln2_g[l]), w_up[l].astype(BF16),
                      conv_w[l], conv_b[l], w_down[l].astype(BF16), seq_len, tm, 256)
    return x2.reshape(batch, seq_len, D_MODEL)
```

```python
import functools

import jax
import jax.numpy as jnp
from jax import lax
from jax.experimental import pallas as pl
from jax.experimental.pallas import tpu as pltpu

F32 = jnp.float32
BF16 = jnp.bfloat16

D_MODEL = 1024
HEAD_DIM = 64
RWKV_HEADS = 6
RWKV_WIDTH = RWKV_HEADS * HEAD_DIM
SB_HEADS = 6
SB_WIDTH = SB_HEADS * HEAD_DIM
POOL_WINDOWS = (2, 4, 8, 16)
POOL_WIDTH = D_MODEL - RWKV_WIDTH - SB_WIDTH
POOL_GROUP_DIM = POOL_WIDTH // len(POOL_WINDOWS)
DECAY_LORA = 64
AAA_LORA = 64
GATE_LORA = 128
MV_LORA = 32
RWKV_COLS = 3 * RWKV_WIDTH + DECAY_LORA + AAA_LORA + GATE_LORA
SB_COLS = 3 * SB_WIDTH
IN_COLS = RWKV_COLS + SB_COLS + POOL_WIDTH
D_FF = 2816
NORM_EPS = 1e-6
LN_X_EPS = 64e-5
L2_EPS = 1e-12
LOG2_E = 1.4426950408889634

SUBLANES = 8
LANES = 128
BF16_ROWS = 2 * SUBLANES
PAIR = 2 * HEAD_DIM
RWKV_PAIRS = RWKV_WIDTH // PAIR
SB_PAIRS = SB_WIDTH // PAIR
CHUNK = 64
POOL_HALO = 16
KEY_BLOCKS = 8
VMEM_LIMIT = 48 * 1024 * 1024

NN = (((1,), (0,)), ((), ()))
NT = (((1,), (1,)), ((), ()))
TN = (((0,), (0,)), ((), ()))


def _cparams(*sem):
    return pltpu.CompilerParams(dimension_semantics=sem, vmem_limit_bytes=VMEM_LIMIT)


def _dot(a, b, dims=NN):
    return lax.dot_general(a, b, dims, preferred_element_type=F32)


def _split2(x):
    hi = x.astype(BF16)
    lo = (x - hi.astype(F32)).astype(BF16)
    return hi, lo


def _neg_abs(x):
    return lax.bitcast_convert_type(lax.bitcast_convert_type(x, jnp.int32) | jnp.int32(-2 ** 31), F32)


def _split3(x):
    hi = x.astype(BF16)
    r1 = x - hi.astype(F32)
    mid = r1.astype(BF16)
    lo = (r1 - mid.astype(F32)).astype(BF16)
    return hi, mid, lo


def _head_reduce(a, indicator_bf16):
    return _dot(a.astype(BF16), indicator_bf16)


def _dot_exact_rhs(a, b_bf16, dims=NN):
    h, l = _split2(a)
    return _dot(h, b_bf16, dims) + _dot(l, b_bf16, dims)


def _dot_exact_lhs(a_bf16, b, dims=NN):
    h, m, l = _split3(b)
    return _dot(a_bf16, h, dims) + (_dot(a_bf16, m, dims) + _dot(a_bf16, l, dims))


def _sigmoid(x):
    return 1.0 / (1.0 + jnp.exp(-x))


def _softplus(x):
    return jnp.maximum(x, 0.0) + jnp.log(1.0 + jnp.exp(-jnp.abs(x)))


def _rms_rows(x, g):
    return x * lax.rsqrt(jnp.mean(x * x, axis=-1, keepdims=True) + NORM_EPS) * g


def _front_kernel(blocks_per_seq, has_vmix, *refs):
    refs = list(refs)
    x_ref, xh_ref, ln_ref, win_ref = refs[:4]
    mu_ref, w0_ref, w2_ref, a0_ref, a2_ref, g2_ref, kk_ref, ka_ref, rk_ref, hsum_ref = refs[4:14]
    refs = refs[14:]
    if has_vmix:
        vf_ref, v0_ref, v1_ref, v2_ref = refs[:4]
        refs = refs[4:]
    qg_ref, kg_ref, hmean_ref, pw_ref, pb_ref, ps_ref = refs[:6]
    r_o, lw_o, k_o, v_o, kk_o, a_o, g_o, bonus_o, q_o, kn_o, vb_o, pool_o = refs[6:]

    tm = x_ref.shape[0]
    blk = pl.program_id(0) % blocks_per_seq
    first = blk == 0
    ln = ln_ref[...]
    w_in = win_ref[...]
    proj = _dot(_rms_rows(x_ref[...], ln).astype(BF16), w_in)
    proj_prev = jnp.where(first, 0.0, _dot(_rms_rows(xh_ref[...], ln).astype(BF16), w_in))

    p = proj[:, :RWKV_COLS]
    prev_last = proj_prev[POOL_HALO - 1:POOL_HALO, :RWKV_COLS]
    row = lax.broadcasted_iota(jnp.int32, (tm, 1), 0)
    shifted = jnp.where(row == 0, prev_last, pltpu.roll(p, 1, axis=0))
    p = p + (shifted - p) * mu_ref[...]

    c0, c1, c2 = RWKV_WIDTH, 2 * RWKV_WIDTH, 3 * RWKV_WIDTH
    r = p[:, 0:c0]
    k = p[:, c0:c1]
    v = p[:, c1:c2]
    xwa = p[:, c2:c2 + DECAY_LORA + AAA_LORA]
    xw = xwa[:, :DECAY_LORA]
    xa = xwa[:, DECAY_LORA:]
    xg = p[:, c2 + DECAY_LORA + AAA_LORA:]

    lora = lambda t, w_ref: _dot(t.astype(BF16), w_ref[...])
    w = -_softplus(-(w0_ref[...] + lora(jnp.tanh(xw), w2_ref))) - 0.5
    lw_o[...] = -jnp.exp(w)
    a = _sigmoid(a0_ref[...] + lora(xa, a2_ref))
    g_o[...] = lora(_sigmoid(xg), g2_ref)
    if has_vmix:
        mix = _sigmoid(v0_ref[...] + lora(lora(v, v1_ref), v2_ref))
        v = v + (vf_ref[...] - v) * mix
    hsum = hsum_ref[...]
    kk = k * kk_ref[...]
    kk = kk / jnp.maximum(jnp.sqrt(_head_reduce(kk * kk, hsum)), L2_EPS)
    k = k * (1.0 + (a - 1.0) * ka_ref[...])
    r_o[...] = r
    k_o[...] = k
    v_o[...] = v
    kk_o[...] = kk
    a_o[...] = a
    bonus_o[...] = _head_reduce(r * k * rk_ref[...], hsum) * v

    hmean = hmean_ref[...]
    lane_lo = lax.broadcasted_iota(jnp.int32, (tm, PAIR), 1) < HEAD_DIM
    for pair in range(SB_PAIRS):
        qc = RWKV_COLS + pair * PAIR
        q = proj[:, qc:qc + PAIR]
        kx = proj[:, qc + SB_WIDTH:qc + SB_WIDTH + PAIR]
        qn = q * lax.rsqrt(_head_reduce(q * q, hmean) + NORM_EPS) * qg_ref[...] * (LOG2_E * HEAD_DIM ** -0.5)
        kn = kx * lax.rsqrt(_head_reduce(kx * kx, hmean) + NORM_EPS) * kg_ref[...]
        q_o[:, 2 * pair * PAIR:(2 * pair + 1) * PAIR] = jnp.where(lane_lo, qn, 0.0).astype(BF16)
        q_o[:, (2 * pair + 1) * PAIR:(2 * pair + 2) * PAIR] = jnp.where(lane_lo, 0.0, qn).astype(BF16)
        kn_o[:, pair * PAIR:(pair + 1) * PAIR] = kn.astype(BF16)
    vb_o[...] = proj[:, RWKV_COLS + 2 * SB_WIDTH:RWKV_COLS + SB_COLS].astype(BF16)

    u = proj[:, RWKV_COLS + SB_COLS:]
    ext = jnp.concatenate([proj_prev[:, RWKV_COLS + SB_COLS:], u], axis=0)
    pos = (blk * tm + row).astype(F32)
    group = lax.broadcasted_iota(jnp.int32, (tm, POOL_WIDTH), 1) // POOL_GROUP_DIM
    acc = ext
    span = 1
    pooled = jnp.zeros((tm, POOL_WIDTH), F32)
    for gi, win in enumerate(POOL_WINDOWS):
        while span < win:
            acc = acc + pltpu.roll(acc, span, axis=0)
            span *= 2
        count = jnp.minimum(pos + 1.0, float(win))
        pooled = jnp.where(group == gi, acc[POOL_HALO:] / count, pooled)
    pooled = pooled - u
    pool_o[...] = ((_dot(pooled.astype(BF16), pw_ref[...]) + pb_ref[...]) * ps_ref[...]).astype(BF16)


def _front(x2, ln, w_in_bf16, rwkv_params, vmix, sb_params, pool_params, seq_len, tm):
    n = x2.shape[0]
    has_vmix = vmix is not None
    row = lambda i: (i, 0)
    const = lambda i: (0, 0)
    halo_idx = lambda i: (jnp.maximum(i * (tm // POOL_HALO) - 1, 0), 0)
    full = lambda a: pl.BlockSpec(a.shape, const)
    in_specs = [pl.BlockSpec((tm, D_MODEL), row),
                pl.BlockSpec((POOL_HALO, D_MODEL), halo_idx),
                full(ln),
                pl.BlockSpec((D_MODEL, IN_COLS), const, pipeline_mode=pl.Buffered(1))]
    in_specs += [full(a) for a in rwkv_params]
    args = [x2, x2, ln, w_in_bf16] + list(rwkv_params)
    if has_vmix:
        in_specs += [pl.BlockSpec((tm, RWKV_WIDTH), row)] + [full(a) for a in vmix[1:]]
        args += list(vmix)
    in_specs += [full(a) for a in sb_params + pool_params]
    args += list(sb_params + pool_params)
    tok = lambda width: pl.BlockSpec((tm, width), row)
    shape = lambda width, dtype: jax.ShapeDtypeStruct((n, width), dtype)
    return pl.pallas_call(
        functools.partial(_front_kernel, seq_len // tm, has_vmix),
        grid=(n // tm,),
        in_specs=in_specs,
        out_specs=[tok(RWKV_WIDTH)] * 8 + [tok(2 * SB_WIDTH), tok(SB_WIDTH), tok(SB_WIDTH), tok(POOL_WIDTH)],
        out_shape=[shape(RWKV_WIDTH, F32)] * 8 + [shape(2 * SB_WIDTH, BF16), shape(SB_WIDTH, BF16),
                                                  shape(SB_WIDTH, BF16), shape(POOL_WIDTH, BF16)],
        compiler_params=_cparams("parallel"),
        name="front",
    )(*args)


def _pair_rows(x, lane_lo):
    return jnp.concatenate([jnp.where(lane_lo, x, 0.0), jnp.where(lane_lo, 0.0, x)], axis=0)


def _rwkv_chunk_kernel(chunks, r_ref, lw_ref, k_ref, v_ref, kk_ref, a_ref, tri_ref,
                       q1_o, q2_o, g_o, e_o):
    two = 2 * CHUNK
    ri = lax.broadcasted_iota(jnp.int32, (two, two), 0)
    ci = lax.broadcasted_iota(jnp.int32, (two, two), 1)
    same = (ri // CHUNK) == (ci // CHUNK)
    strict = same & ((ci % CHUNK) < (ri % CHUNK))
    incl = same & ((ci % CHUNK) <= (ri % CHUNK))
    eye = (ri == ci).astype(F32)
    lane_lo = lax.broadcasted_iota(jnp.int32, (CHUNK, PAIR), 1) < HEAD_DIM
    tri = tri_ref[...]

    units = []
    for c in range(chunks):
        rows = pl.ds(c * CHUNK, CHUNK)
        lw = lw_ref[rows, :]
        cl = _dot_exact_lhs(tri, lw)
        e_pos = jnp.exp(cl)
        e_prev = jnp.exp(cl - lw)
        e_neg = jnp.exp(-cl)
        kk = kk_ref[rows, :]
        at = -kk * e_prev
        bt = kk * a_ref[rows, :] * e_neg
        rt = r_ref[rows, :] * e_pos
        kt = k_ref[rows, :] * e_neg
        v = v_ref[rows, :]
        wc = e_pos[CHUNK - 1:CHUNK, :]
        for p in range(RWKV_PAIRS):
            ls = slice(p * PAIR, (p + 1) * PAIR)
            units.append(dict(
                c=c, p=p, rows=rows, ls=ls, wc=wc[:, ls],
                xa=_pair_rows(at[:, ls], lane_lo).astype(BF16), xr=_pair_rows(rt[:, ls], lane_lo),
                vv=_pair_rows(v[:, ls], lane_lo).astype(BF16), v=v[:, ls].astype(BF16),
                bt=bt[:, ls].astype(BF16), kt=kt[:, ls].astype(BF16)))

    for u in units:
        sc = _dot(jnp.concatenate([u["xa"], u["xr"].astype(BF16)], axis=0),
                  jnp.concatenate([u["bt"], u["bt"], u["kt"], u["kt"]], axis=0), NT)
        a_ab = jnp.where(strict, sc[:two, :two], 0.0)
        u["a_ak"] = jnp.where(strict, sc[:two, two:], 0.0).astype(BF16)
        u["m"] = jnp.concatenate([jnp.where(incl, sc[two:, :two], 0.0),
                                  jnp.where(incl, sc[two:, two:], 0.0)], axis=1).astype(BF16)
        u["t"] = eye + a_ab
        u["pw"] = a_ab.astype(BF16)

    for _ in range(CHUNK.bit_length() - 2):
        for u in units:
            u["pw"] = _dot(u["pw"], u["pw"]).astype(BF16)
        for u in units:
            u["t"] = u["t"] + _dot(u["t"].astype(BF16), u["pw"])

    for u in units:
        u["z"] = _dot(u["a_ak"], u["vv"]).astype(BF16)
    for u in units:
        u["pm"] = _dot(u["t"].astype(BF16), jnp.concatenate([u["xa"], u["z"]], axis=1))
    for u in units:
        vv = u["vv"]
        qm = _dot(u["m"], jnp.concatenate([u["pm"].astype(BF16),
                                           jnp.concatenate([jnp.zeros_like(vv), vv], axis=1)], axis=0))
        q1m = u["xr"] + qm[:, :PAIR]
        q2m = qm[:, PAIR:]
        q1_o[u["rows"], u["ls"]] = (q1m[:CHUNK] + q1m[CHUNK:]).astype(BF16)
        q2_o[u["rows"], u["ls"]] = q2m[:CHUNK] + q2m[CHUNK:]
    for u in units:
        pm = u["pm"]
        p12 = (pm[:CHUNK] + pm[CHUNK:]).astype(BF16)
        lt = jnp.concatenate([p12, jnp.concatenate([jnp.zeros_like(u["bt"]), u["v"]], axis=1)], axis=0)
        ge = _dot(lt, jnp.concatenate([u["bt"], u["kt"]], axis=0), TN)
        g_o[u["c"], u["p"]] = jnp.where(same, (eye + ge[:PAIR]) * u["wc"], 0.0).astype(BF16)
        e_o[u["c"], u["p"]] = jnp.where(same, ge[PAIR:] * u["wc"], 0.0)


def _rwkv_chunk(r, lw, k, v, kk, a, tri, chunks):
    n = r.shape[0]
    tc = chunks * CHUNK
    row = lambda i: (i, 0)
    tok = pl.BlockSpec((tc, RWKV_WIDTH), row)
    mat = pl.BlockSpec((chunks, RWKV_PAIRS, PAIR, PAIR), lambda i: (i, 0, 0, 0))
    tok_shape = lambda dtype: jax.ShapeDtypeStruct((n, RWKV_WIDTH), dtype)
    mat_shape = lambda dtype: jax.ShapeDtypeStruct((n // CHUNK, RWKV_PAIRS, PAIR, PAIR), dtype)
    return pl.pallas_call(
        functools.partial(_rwkv_chunk_kernel, chunks),
        grid=(n // tc,),
        in_specs=[tok] * 6 + [pl.BlockSpec((CHUNK, CHUNK), lambda i: (0, 0))],
        out_specs=[tok, tok, mat, mat],
        out_shape=[tok_shape(BF16), tok_shape(F32), mat_shape(BF16), mat_shape(F32)],
        compiler_params=_cparams("parallel"),
        name="rwkv_chunk",
    )(r, lw, k, v, kk, a, tri)


def _rwkv_scan_kernel(chunks, q1_ref, q2_ref, g_ref, e_ref, bonus_ref, gate_ref, hmean_ref,
                      lnw_ref, lnb_ref, o_ref, s_ref, y_ref):
    @pl.when(pl.program_id(1) == 0)
    def _():
        s_ref[...] = jnp.zeros_like(s_ref)

    for c in range(chunks):
        rows = pl.ds(c * CHUNK, CHUNK)
        for p in range(RWKV_PAIRS):
            ls = slice(p * PAIR, (p + 1) * PAIR)
            s = s_ref[p]
            s_hi, s_lo = _split2(s)
            y_ref[rows, ls] = _dot(q1_ref[rows, ls], s_hi, NT) + q2_ref[rows, ls]
            g = g_ref[c, p]
            s_ref[p] = _dot(s_hi, g) + (_dot(s_lo, g) + e_ref[c, p])

    y = y_ref[...]
    hmean = hmean_ref[...]
    mean = _dot_exact_rhs(y, hmean)
    yc = y - mean
    var = _dot_exact_rhs(yc * yc, hmean)
    yn = yc * lax.rsqrt(var + LN_X_EPS) * lnw_ref[...] + lnb_ref[...]
    o_ref[...] = ((yn + bonus_ref[...]) * gate_ref[...]).astype(o_ref.dtype)


def _rwkv_scan(q1, q2, g, e, bonus, gate, hmean, lnw, lnb, batch, seq_len, chunks):
    n = q1.shape[0]
    tc = chunks * CHUNK
    steps = seq_len // tc
    row = lambda b, i: (b * steps + i, 0)
    const = lambda b, i: (0, 0)
    tok = pl.BlockSpec((tc, RWKV_WIDTH), row)
    mat = pl.BlockSpec((chunks, RWKV_PAIRS, PAIR, PAIR), lambda b, i: (b * steps + i, 0, 0, 0))
    vec = pl.BlockSpec((1, RWKV_WIDTH), const)
    return pl.pallas_call(
        functools.partial(_rwkv_scan_kernel, chunks),
        grid=(batch, steps),
        in_specs=[tok, tok, mat, mat, tok, tok,
                  pl.BlockSpec((RWKV_WIDTH, RWKV_WIDTH), const), vec, vec],
        out_specs=tok,
        out_shape=jax.ShapeDtypeStruct((n, RWKV_WIDTH), BF16),
        scratch_shapes=[pltpu.VMEM((RWKV_PAIRS, PAIR, PAIR), F32),
                        pltpu.VMEM((tc, RWKV_WIDTH), F32)],
        compiler_params=_cparams("parallel", "arbitrary"),
        name="rwkv_scan",
    )(q1, q2, g, e, bonus, gate, hmean, lnw, lnb)


def _sb_attn_kernel(tq, tk, q_ref, k_ref, v_ref, upper_ref, o_ref):
    qi = pl.program_id(2)
    sub = tq // tk
    neg_upper = upper_ref[...]
    first_row = [(sub - 1 - d) * tk for d in range(sub)]
    causal_of = [lax.broadcasted_iota(jnp.int32, (tq - r0, tk), 1) < lax.broadcasted_iota(jnp.int32, (tq - r0, tk), 0)
                 for r0 in first_row]

    def key_rows(j):
        return pl.ds(pl.multiple_of(j * tk, tk), tk)

    def consume(top, size, carry, diagonal_first):
        tiles = [(h, key_rows(top - d)) + ((first_row[d], causal_of[d]) if diagonal_first and d < sub else (0, None))
                 for d in range(size) for h in range(2)]
        carry = list(carry)
        log_betas, sps16, totals, logits = {}, {}, {}, {}
        for step in range(len(tiles) + 2):
            t = step
            if t < len(tiles):
                h, ks, r0, causal = tiles[t]
                z = _dot(q_ref[r0:, h * PAIR:(h + 1) * PAIR], k_ref[ks, :], NT)
                sp = jnp.maximum(z, 0.0) + jnp.log(1.0 + jnp.exp2(_neg_abs(z))) * LOG2_E
                if causal is not None:
                    sp = jnp.where(causal, sp, 0.0)
                sps16[t] = sp.astype(BF16)
                log_betas[t] = z - sp
                totals[t] = z[:, 0:1]
            t = step - 1
            if 0 <= t < len(tiles):
                logits[t] = log_betas.pop(t) + _dot(sps16.pop(t), neg_upper)
                totals[t] = totals[t] - logits[t][:, 0:1]
            t = step - 2
            if 0 <= t < len(tiles):
                h, ks, r0, causal = tiles[t]
                used, acc = carry[h]
                w = jnp.exp2(logits.pop(t) - used[r0:])
                if causal is not None:
                    w = jnp.where(causal, w, 0.0)
                used_new = used[r0:] + totals.pop(t)
                acc_new = acc[r0:] + _dot(w.astype(BF16), v_ref[ks, :])
                if r0:
                    used_new = jnp.concatenate([used[:r0], used_new], axis=0)
                    acc_new = jnp.concatenate([acc[:r0], acc_new], axis=0)
                carry[h] = (used_new, acc_new)
        return tuple(carry)

    zero = (jnp.zeros((tq, 1), F32), jnp.zeros((tq, PAIR), F32))
    below = qi * sub
    extra = below % KEY_BLOCKS
    top = below + sub - 1

    def first_group(e):
        if e + sub >= KEY_BLOCKS:
            return lambda c: consume(top, sub + e, c, True)
        return lambda c: lax.cond(extra == e, lambda c2: consume(top, sub + e, c2, True), first_group(e + sub), c)

    carry = first_group(0)((zero, zero))
    carry = lax.fori_loop(0, below // KEY_BLOCKS,
                          lambda i, c: consume(below - extra - 1 - KEY_BLOCKS * i, KEY_BLOCKS, c, False), carry)
    lane_lo = lax.broadcasted_iota(jnp.int32, (tq, PAIR), 1) < HEAD_DIM
    o_ref[...] = jnp.where(lane_lo, carry[0][1], carry[1][1]).astype(o_ref.dtype)


def _sb_attn(qm, kn, vb, upper, batch, seq_len, tq, tk):
    n = qm.shape[0]
    steps = seq_len // tq
    assert tq % tk == 0 and KEY_BLOCKS % (tq // tk) == 0
    q_spec = pl.BlockSpec((tq, 2 * PAIR), lambda b, p, i: (b * steps + i, p))
    k_spec = pl.BlockSpec((seq_len, PAIR), lambda b, p, i: (b, p))
    return pl.pallas_call(
        functools.partial(_sb_attn_kernel, tq, tk),
        grid=(batch, SB_PAIRS, steps),
        in_specs=[q_spec, k_spec, k_spec,
                  pl.BlockSpec((tk, tk), lambda b, p, i: (0, 0))],
        out_specs=pl.BlockSpec((tq, PAIR), lambda b, p, i: (b * steps + i, p)),
        out_shape=jax.ShapeDtypeStruct((n, SB_WIDTH), BF16),
        compiler_params=_cparams("parallel", "parallel", "arbitrary"),
        name="sb_attn",
    )(qm, kn, vb, upper)


def _mix_ffn_kernel(blocks_per_seq, tf, x_ref, xh_ref, yr_ref, yrh_ref, ys_ref, ysh_ref, yp_ref, yph_ref,
                    wr_ref, ws_ref, wp_ref, g_ref, wup_ref, cw_ref, cb_ref, wd_ref, o_ref, h_ref, act_ref):
    first = (pl.program_id(0) % blocks_per_seq) == 0
    g = g_ref[...]

    def mixed(x, yr, ys, yp):
        return x + (_dot(yr, wr_ref[...]) + (_dot(ys, ws_ref[...]) + _dot(yp, wp_ref[...])))

    x = mixed(x_ref[...], yr_ref[...], ys_ref[...], yp_ref[...])
    tail = slice(BF16_ROWS - SUBLANES, BF16_ROWS)
    x_prev = mixed(xh_ref[...], yrh_ref[tail, :], ysh_ref[tail, :], yph_ref[tail, :])
    h_ref[0:SUBLANES, :] = jnp.where(first, 0.0, _rms_rows(x_prev, g)).astype(BF16)
    h_ref[SUBLANES:, :] = _rms_rows(x, g).astype(BF16)
    h = h_ref[...]

    def conv(col):
        cols = slice(col, col + tf)
        up = _dot(h, wup_ref[:, cols])
        cw = cw_ref[:, cols]
        c = cb_ref[:, cols] + up[SUBLANES:] * cw[2:3]
        c = c + pltpu.roll(up, 1, axis=0)[SUBLANES:] * cw[1:2]
        return c + pltpu.roll(up, 2, axis=0)[SUBLANES:] * cw[0:1]

    for j in range(D_FF // tf):
        gate = conv(j * tf)
        val = conv(D_FF + j * tf)
        act_ref[:, j * tf:(j + 1) * tf] = (gate * _sigmoid(gate) * val).astype(BF16)
    o_ref[...] = x + _dot(act_ref[...], wd_ref[...])


def _mix_ffn(x2, yr, ys, yp, w_out_bf16, g, w_up_bf16, conv_w, conv_b, w_down_bf16, seq_len, tm, tf):
    n = x2.shape[0]
    row = lambda i: (i, 0)
    const = lambda i: (0, 0)
    halo = lambda rows: (lambda i: (jnp.maximum(i * (tm // rows) - 1, 0), 0))
    resident = lambda shape: pl.BlockSpec(shape, const, pipeline_mode=pl.Buffered(1))

    def with_halo(width, rows):
        return [pl.BlockSpec((tm, width), row), pl.BlockSpec((rows, width), halo(rows))]

    wr = w_out_bf16[:RWKV_WIDTH]
    ws = w_out_bf16[RWKV_WIDTH:RWKV_WIDTH + SB_WIDTH]
    wp = w_out_bf16[RWKV_WIDTH + SB_WIDTH:]
    return pl.pallas_call(
        functools.partial(_mix_ffn_kernel, seq_len // tm, tf),
        grid=(n // tm,),
        in_specs=(with_halo(D_MODEL, SUBLANES) + with_halo(RWKV_WIDTH, BF16_ROWS) + with_halo(SB_WIDTH, BF16_ROWS)
                  + with_halo(POOL_WIDTH, BF16_ROWS)
                  + [resident((RWKV_WIDTH, D_MODEL)), resident((SB_WIDTH, D_MODEL)), resident((POOL_WIDTH, D_MODEL)),
                     pl.BlockSpec((1, D_MODEL), const),
                     resident((D_MODEL, 2 * D_FF)),
                     pl.BlockSpec((3, 2 * D_FF), const),
                     pl.BlockSpec((1, 2 * D_FF), const),
                     resident((D_FF, D_MODEL))]),
        out_specs=pl.BlockSpec((tm, D_MODEL), row),
        out_shape=jax.ShapeDtypeStruct((n, D_MODEL), F32),
        scratch_shapes=[pltpu.VMEM((tm + SUBLANES, D_MODEL), BF16),
                        pltpu.VMEM((tm, D_FF), BF16)],
        compiler_params=_cparams("parallel"),
        name="mix_ffn",
    )(x2, x2, yr, yr, ys, ys, yp, yp, wr, ws, wp, g, w_up_bf16, conv_w, conv_b.reshape(1, 2 * D_FF), w_down_bf16)


def _head_indicator(width, value):
    idx = jnp.arange(width) // HEAD_DIM
    return jnp.where(idx[:, None] == idx[None, :], value, 0.0).astype(BF16)


def kernel(x, ln1_g, w_in, mu_shift, w0, w2, a0, a2, g2, k_k, k_a, r_k, lnx_w, lnx_b, v0, v1, v2, q_gain, k_gain, pool_w, pool_b, pool_scale, w_out, ln2_g, w_up, conv_w, conv_b, w_down):
    batch, seq_len, _ = x.shape
    depth = w_in.shape[0]
    n = batch * seq_len
    tm = min(512, seq_len)
    tq = min(512, seq_len)
    tk = min(256, seq_len)
    scan_chunks = min(8, seq_len // CHUNK)
    xform_chunks = min(4, seq_len // CHUNK)
    assert seq_len % tm == 0 and seq_len % (scan_chunks * CHUNK) == 0 and tm % POOL_HALO == 0

    hsum = _head_indicator(RWKV_WIDTH, 1.0)
    hmean = _head_indicator(RWKV_WIDTH, 1.0 / HEAD_DIM)
    tri = jnp.tril(jnp.ones((CHUNK, CHUNK), F32)).astype(BF16)
    upper = (-jnp.tril(jnp.ones((tk, tk), F32), -1)).astype(BF16)
    row = lambda a: a.reshape(1, -1)

    x2 = x.reshape(n, D_MODEL)
    v_first = None
    for l in range(depth):
        rwkv_params = (row(mu_shift[l]), row(w0[l]), w2[l].astype(BF16), row(a0[l]), a2[l].astype(BF16),
                       g2[l].astype(BF16), row(k_k[l]), row(k_a[l]), row(r_k[l]), hsum)
        vmix = None if l == 0 else (v_first, row(v0[l - 1]), v1[l - 1].astype(BF16), v2[l - 1].astype(BF16))
        sb_params = (row(jnp.tile(q_gain[l], PAIR // HEAD_DIM)), row(jnp.tile(k_gain[l], PAIR // HEAD_DIM)),
                     hmean[:PAIR, :PAIR])
        w_bd = jax.scipy.linalg.block_diag(*[pool_w[l, gi] for gi in range(len(POOL_WINDOWS))])
        pool_params = (w_bd.astype(BF16), row(pool_b[l]), row(pool_scale[l]))
        (r, lw, k, v, kk, a, gate, bonus, qm, kn, vb, y_pool) = _front(
            x2, row(ln1_g[l]), w_in[l].astype(BF16), rwkv_params, vmix, sb_params, pool_params, seq_len, tm)
        if l == 0:
            v_first = v
        q1, q2, g_mat, e_mat = _rwkv_chunk(r, lw, k, v, kk, a, tri, xform_chunks)
        y_rwkv = _rwkv_scan(q1, q2, g_mat, e_mat, bonus, gate, hmean, row(lnx_w[l]), row(lnx_b[l]),
                            batch, seq_len, scan_chunks)
        y_sb = _sb_attn(qm, kn, vb, upper, batch, seq_len, tq, tk)

        x2 = _mix_ffn(x2, y_rwkv, y_sb, y_pool, w_out[l].astype(BF16), row(ln2_g[l]), w_up[l].astype(BF16),
                      conv_w[l], conv_b[l], w_down[l].astype(BF16), seq_len, tm, 256)
    return x2.reshape(batch, seq_len, D_MODEL)
```

```python
import functools

import jax
import jax.numpy as jnp
from jax import lax
from jax.experimental import pallas as pl
from jax.experimental.pallas import tpu as pltpu

F32 = jnp.float32
BF16 = jnp.bfloat16

D_MODEL = 1024
HEAD_DIM = 64
RWKV_HEADS = 6
RWKV_WIDTH = RWKV_HEADS * HEAD_DIM
SB_HEADS = 6
SB_WIDTH = SB_HEADS * HEAD_DIM
POOL_WINDOWS = (2, 4, 8, 16)
POOL_WIDTH = D_MODEL - RWKV_WIDTH - SB_WIDTH
POOL_GROUP_DIM = POOL_WIDTH // len(POOL_WINDOWS)
DECAY_LORA = 64
AAA_LORA = 64
GATE_LORA = 128
MV_LORA = 32
RWKV_COLS = 3 * RWKV_WIDTH + DECAY_LORA + AAA_LORA + GATE_LORA
SB_COLS = 3 * SB_WIDTH
IN_COLS = RWKV_COLS + SB_COLS + POOL_WIDTH
D_FF = 2816
NORM_EPS = 1e-6
LN_X_EPS = 64e-5
L2_EPS = 1e-12
LOG2_E = 1.4426950408889634

SUBLANES = 8
LANES = 128
BF16_ROWS = 2 * SUBLANES
PAIR = 2 * HEAD_DIM
RWKV_PAIRS = RWKV_WIDTH // PAIR
SB_PAIRS = SB_WIDTH // PAIR
CHUNK = 64
POOL_HALO = 16
KEY_BLOCKS = 8
VMEM_LIMIT = 48 * 1024 * 1024

NN = (((1,), (0,)), ((), ()))
NT = (((1,), (1,)), ((), ()))
TN = (((0,), (0,)), ((), ()))


def _cparams(*sem):
    return pltpu.CompilerParams(dimension_semantics=sem, vmem_limit_bytes=VMEM_LIMIT)


def _dot(a, b, dims=NN):
    return lax.dot_general(a, b, dims, preferred_element_type=F32)


def _split2(x):
    hi = x.astype(BF16)
    lo = (x - hi.astype(F32)).astype(BF16)
    return hi, lo


def _neg_abs(x):
    return lax.bitcast_convert_type(lax.bitcast_convert_type(x, jnp.int32) | jnp.int32(-2 ** 31), F32)


def _split3(x):
    hi = x.astype(BF16)
    r1 = x - hi.astype(F32)
    mid = r1.astype(BF16)
    lo = (r1 - mid.astype(F32)).astype(BF16)
    return hi, mid, lo


def _head_reduce(a, indicator_bf16):
    return _dot(a.astype(BF16), indicator_bf16)


def _dot_exact_rhs(a, b_bf16, dims=NN):
    h, l = _split2(a)
    return _dot(h, b_bf16, dims) + _dot(l, b_bf16, dims)


def _dot_exact_lhs(a_bf16, b, dims=NN):
    h, m, l = _split3(b)
    return _dot(a_bf16, h, dims) + (_dot(a_bf16, m, dims) + _dot(a_bf16, l, dims))


def _sigmoid(x):
    return 1.0 / (1.0 + jnp.exp(-x))


def _softplus(x):
    return jnp.maximum(x, 0.0) + jnp.log(1.0 + jnp.exp(-jnp.abs(x)))


def _rms_rows(x, g):
    return x * lax.rsqrt(jnp.mean(x * x, axis=-1, keepdims=True) + NORM_EPS) * g


def _front_kernel(blocks_per_seq, has_vmix, *refs):
    refs = list(refs)
    x_ref, xh_ref, ln_ref, win_ref = refs[:4]
    mu_ref, w0_ref, w2_ref, a0_ref, a2_ref, g2_ref, kk_ref, ka_ref, rk_ref, hsum_ref = refs[4:14]
    refs = refs[14:]
    if has_vmix:
        vf_ref, v0_ref, v1_ref, v2_ref = refs[:4]
        refs = refs[4:]
    qg_ref, kg_ref, hmean_ref, pw_ref, pb_ref, ps_ref = refs[:6]
    r_o, lw_o, k_o, v_o, kk_o, a_o, g_o, bonus_o, q_o, kn_o, vb_o, pool_o = refs[6:]

    tm = x_ref.shape[0]
    blk = pl.program_id(0) % blocks_per_seq
    first = blk == 0
    ln = ln_ref[...]
    w_in = win_ref[...]
    proj = _dot(_rms_rows(x_ref[...], ln).astype(BF16), w_in)
    proj_rest = proj[:, RWKV_COLS:]
    proj_prev = jnp.where(first, 0.0, _dot(_rms_rows(xh_ref[...], ln).astype(BF16), w_in))

    p = proj[:, :RWKV_COLS]
    prev_last = proj_prev[POOL_HALO - 1:POOL_HALO, :RWKV_COLS]
    row = lax.broadcasted_iota(jnp.int32, (tm, 1), 0)
    shifted = jnp.where(row == 0, prev_last, pltpu.roll(p, 1, axis=0))
    p = p + (shifted - p) * mu_ref[...]

    c0, c1, c2 = RWKV_WIDTH, 2 * RWKV_WIDTH, 3 * RWKV_WIDTH
    r = p[:, 0:c0]
    k = p[:, c0:c1]
    v = p[:, c1:c2]
    xwa = p[:, c2:c2 + DECAY_LORA + AAA_LORA]
    xw = xwa[:, :DECAY_LORA]
    xa = xwa[:, DECAY_LORA:]
    xg = p[:, c2 + DECAY_LORA + AAA_LORA:]

    lora = lambda t, w_ref: _dot(t.astype(BF16), w_ref[...])
    w = -_softplus(-(w0_ref[...] + lora(jnp.tanh(xw), w2_ref))) - 0.5
    lw_o[...] = -jnp.exp(w)
    a = _sigmoid(a0_ref[...] + lora(xa, a2_ref))
    g_o[...] = lora(_sigmoid(xg), g2_ref)
    if has_vmix:
        mix = _sigmoid(v0_ref[...] + lora(lora(v, v1_ref), v2_ref))
        v = v + (vf_ref[...] - v) * mix
    hsum = hsum_ref[...]
    kk = k * kk_ref[...]
    kk = kk / jnp.maximum(jnp.sqrt(_head_reduce(kk * kk, hsum)), L2_EPS)
    k = k * (1.0 + (a - 1.0) * ka_ref[...])
    r_o[...] = r
    k_o[...] = k
    v_o[...] = v
    kk_o[...] = kk
    a_o[...] = a
    bonus_o[...] = _head_reduce(r * k * rk_ref[...], hsum) * v

    hmean = hmean_ref[...]
    lane_lo = lax.broadcasted_iota(jnp.int32, (tm, PAIR), 1) < HEAD_DIM
    for pair in range(SB_PAIRS):
        qc = pair * PAIR
        q = proj_rest[:, qc:qc + PAIR]
        kx = proj_rest[:, qc + SB_WIDTH:qc + SB_WIDTH + PAIR]
        qn = q * lax.rsqrt(_head_reduce(q * q, hmean) + NORM_EPS) * qg_ref[...] * (LOG2_E * HEAD_DIM ** -0.5)
        kn = kx * lax.rsqrt(_head_reduce(kx * kx, hmean) + NORM_EPS) * kg_ref[...]
        q_o[:, 2 * pair * PAIR:(2 * pair + 1) * PAIR] = jnp.where(lane_lo, qn, 0.0).astype(BF16)
        q_o[:, (2 * pair + 1) * PAIR:(2 * pair + 2) * PAIR] = jnp.where(lane_lo, 0.0, qn).astype(BF16)
        kn_o[:, pair * PAIR:(pair + 1) * PAIR] = kn.astype(BF16)
    vb_o[...] = proj_rest[:, 2 * SB_WIDTH:SB_COLS].astype(BF16)

    u = proj_rest[:, SB_COLS:]
    ext = jnp.concatenate([proj_prev[:, RWKV_COLS + SB_COLS:], u], axis=0)
    pos = (blk * tm + row).astype(F32)
    group = lax.broadcasted_iota(jnp.int32, (tm, POOL_WIDTH), 1) // POOL_GROUP_DIM
    acc = ext
    span = 1
    pooled = jnp.zeros((tm, POOL_WIDTH), F32)
    for gi, win in enumerate(POOL_WINDOWS):
        while span < win:
            acc = acc + pltpu.roll(acc, span, axis=0)
            span *= 2
        count = jnp.minimum(pos + 1.0, float(win))
        pooled = jnp.where(group == gi, acc[POOL_HALO:] / count, pooled)
    pooled = pooled - u
    pool_o[...] = ((_dot(pooled.astype(BF16), pw_ref[...]) + pb_ref[...]) * ps_ref[...]).astype(BF16)


def _front(x2, ln, w_in_bf16, rwkv_params, vmix, sb_params, pool_params, seq_len, tm):
    n = x2.shape[0]
    has_vmix = vmix is not None
    row = lambda i: (i, 0)
    const = lambda i: (0, 0)
    halo_idx = lambda i: (jnp.maximum(i * (tm // POOL_HALO) - 1, 0), 0)
    full = lambda a: pl.BlockSpec(a.shape, const)
    in_specs = [pl.BlockSpec((tm, D_MODEL), row),
                pl.BlockSpec((POOL_HALO, D_MODEL), halo_idx),
                full(ln),
                pl.BlockSpec((D_MODEL, IN_COLS), const, pipeline_mode=pl.Buffered(1))]
    in_specs += [full(a) for a in rwkv_params]
    args = [x2, x2, ln, w_in_bf16] + list(rwkv_params)
    if has_vmix:
        in_specs += [pl.BlockSpec((tm, RWKV_WIDTH), row)] + [full(a) for a in vmix[1:]]
        args += list(vmix)
    in_specs += [full(a) for a in sb_params + pool_params]
    args += list(sb_params + pool_params)
    tok = lambda width: pl.BlockSpec((tm, width), row)
    shape = lambda width, dtype: jax.ShapeDtypeStruct((n, width), dtype)
    return pl.pallas_call(
        functools.partial(_front_kernel, seq_len // tm, has_vmix),
        grid=(n // tm,),
        in_specs=in_specs,
        out_specs=[tok(RWKV_WIDTH)] * 8 + [tok(2 * SB_WIDTH), tok(SB_WIDTH), tok(SB_WIDTH), tok(POOL_WIDTH)],
        out_shape=[shape(RWKV_WIDTH, F32)] * 8 + [shape(2 * SB_WIDTH, BF16), shape(SB_WIDTH, BF16),
                                                  shape(SB_WIDTH, BF16), shape(POOL_WIDTH, BF16)],
        compiler_params=_cparams("parallel"),
        name="front",
    )(*args)


def _pair_rows(x, lane_lo):
    return jnp.concatenate([jnp.where(lane_lo, x, 0.0), jnp.where(lane_lo, 0.0, x)], axis=0)


def _rwkv_chunk_kernel(chunks, r_ref, lw_ref, k_ref, v_ref, kk_ref, a_ref, tri_ref,
                       q1_o, q2_o, g_o, e_o):
    two = 2 * CHUNK
    ri = lax.broadcasted_iota(jnp.int32, (two, two), 0)
    ci = lax.broadcasted_iota(jnp.int32, (two, two), 1)
    same = (ri // CHUNK) == (ci // CHUNK)
    strict = same & ((ci % CHUNK) < (ri % CHUNK))
    incl = same & ((ci % CHUNK) <= (ri % CHUNK))
    eye = (ri == ci).astype(F32)
    lane_lo = lax.broadcasted_iota(jnp.int32, (CHUNK, PAIR), 1) < HEAD_DIM
    tri = tri_ref[...]

    units = []
    for c in range(chunks):
        rows = pl.ds(c * CHUNK, CHUNK)
        lw = lw_ref[rows, :]
        cl = _dot_exact_lhs(tri, lw)
        e_pos = jnp.exp(cl)
        e_prev = jnp.exp(cl - lw)
        e_neg = jnp.exp(-cl)
        kk = kk_ref[rows, :]
        at = -kk * e_prev
        bt = kk * a_ref[rows, :] * e_neg
        rt = r_ref[rows, :] * e_pos
        kt = k_ref[rows, :] * e_neg
        v = v_ref[rows, :]
        wc = e_pos[CHUNK - 1:CHUNK, :]
        for p in range(RWKV_PAIRS):
            ls = slice(p * PAIR, (p + 1) * PAIR)
            units.append(dict(
                c=c, p=p, rows=rows, ls=ls, wc=wc[:, ls],
                xa=_pair_rows(at[:, ls], lane_lo).astype(BF16), xr=_pair_rows(rt[:, ls], lane_lo),
                vv=_pair_rows(v[:, ls], lane_lo).astype(BF16), v=v[:, ls].astype(BF16),
                bt=bt[:, ls].astype(BF16), kt=kt[:, ls].astype(BF16)))

    for u in units:
        sc = _dot(jnp.concatenate([u["xa"], u["xr"].astype(BF16)], axis=0),
                  jnp.concatenate([u["bt"], u["bt"], u["kt"], u["kt"]], axis=0), NT)
        a_ab = jnp.where(strict, sc[:two, :two], 0.0)
        u["a_ak"] = jnp.where(strict, sc[:two, two:], 0.0).astype(BF16)
        u["m"] = jnp.concatenate([jnp.where(incl, sc[two:, :two], 0.0),
                                  jnp.where(incl, sc[two:, two:], 0.0)], axis=1).astype(BF16)
        u["t"] = eye + a_ab
        u["pw"] = a_ab.astype(BF16)

    for _ in range(CHUNK.bit_length() - 2):
        for u in units:
            u["pw"] = _dot(u["pw"], u["pw"]).astype(BF16)
        for u in units:
            u["t"] = u["t"] + _dot(u["t"].astype(BF16), u["pw"])

    for u in units:
        u["z"] = _dot(u["a_ak"], u["vv"]).astype(BF16)
    for u in units:
        u["pm"] = _dot(u["t"].astype(BF16), jnp.concatenate([u["xa"], u["z"]], axis=1))
    for u in units:
        vv = u["vv"]
        qm = _dot(u["m"], jnp.concatenate([u["pm"].astype(BF16),
                                           jnp.concatenate([jnp.zeros_like(vv), vv], axis=1)], axis=0))
        q1m = u["xr"] + qm[:, :PAIR]
        q2m = qm[:, PAIR:]
        q1_o[u["rows"], u["ls"]] = (q1m[:CHUNK] + q1m[CHUNK:]).astype(BF16)
        q2_o[u["rows"], u["ls"]] = q2m[:CHUNK] + q2m[CHUNK:]
    for u in units:
        pm = u["pm"]
        p12 = (pm[:CHUNK] + pm[CHUNK:]).astype(BF16)
        lt = jnp.concatenate([p12, jnp.concatenate([jnp.zeros_like(u["bt"]), u["v"]], axis=1)], axis=0)
        ge = _dot(lt, jnp.concatenate([u["bt"], u["kt"]], axis=0), TN)
        g_o[u["c"], u["p"]] = jnp.where(same, (eye + ge[:PAIR]) * u["wc"], 0.0).astype(BF16)
        e_o[u["c"], u["p"]] = jnp.where(same, ge[PAIR:] * u["wc"], 0.0)


def _rwkv_chunk(r, lw, k, v, kk, a, tri, chunks):
    n = r.shape[0]
    tc = chunks * CHUNK
    row = lambda i: (i, 0)
    tok = pl.BlockSpec((tc, RWKV_WIDTH), row)
    mat = pl.BlockSpec((chunks, RWKV_PAIRS, PAIR, PAIR), lambda i: (i, 0, 0, 0))
    tok_shape = lambda dtype: jax.ShapeDtypeStruct((n, RWKV_WIDTH), dtype)
    mat_shape = lambda dtype: jax.ShapeDtypeStruct((n // CHUNK, RWKV_PAIRS, PAIR, PAIR), dtype)
    return pl.pallas_call(
        functools.partial(_rwkv_chunk_kernel, chunks),
        grid=(n // tc,),
        in_specs=[tok] * 6 + [pl.BlockSpec((CHUNK, CHUNK), lambda i: (0, 0))],
        out_specs=[tok, tok, mat, mat],
        out_shape=[tok_shape(BF16), tok_shape(F32), mat_shape(BF16), mat_shape(F32)],
        compiler_params=_cparams("parallel"),
        name="rwkv_chunk",
    )(r, lw, k, v, kk, a, tri)


def _rwkv_scan_kernel(chunks, q1_ref, q2_ref, g_ref, e_ref, bonus_ref, gate_ref, hmean_ref,
                      lnw_ref, lnb_ref, o_ref, s_ref, y_ref):
    @pl.when(pl.program_id(1) == 0)
    def _():
        s_ref[...] = jnp.zeros_like(s_ref)

    for c in range(chunks):
        rows = pl.ds(c * CHUNK, CHUNK)
        for p in range(RWKV_PAIRS):
            ls = slice(p * PAIR, (p + 1) * PAIR)
            s = s_ref[p]
            s_hi, s_lo = _split2(s)
            y_ref[rows, ls] = _dot(q1_ref[rows, ls], s_hi, NT) + q2_ref[rows, ls]
            g = g_ref[c, p]
            s_ref[p] = _dot(s_hi, g) + (_dot(s_lo, g) + e_ref[c, p])

    y = y_ref[...]
    hmean = hmean_ref[...]
    mean = _dot_exact_rhs(y, hmean)
    yc = y - mean
    var = _head_reduce(yc * yc, hmean)
    yn = yc * lax.rsqrt(var + LN_X_EPS) * lnw_ref[...] + lnb_ref[...]
    o_ref[...] = ((yn + bonus_ref[...]) * gate_ref[...]).astype(o_ref.dtype)


def _rwkv_scan(q1, q2, g, e, bonus, gate, hmean, lnw, lnb, batch, seq_len, chunks):
    n = q1.shape[0]
    tc = chunks * CHUNK
    steps = seq_len // tc
    row = lambda b, i: (b * steps + i, 0)
    const = lambda b, i: (0, 0)
    tok = pl.BlockSpec((tc, RWKV_WIDTH), row)
    mat = pl.BlockSpec((chunks, RWKV_PAIRS, PAIR, PAIR), lambda b, i: (b * steps + i, 0, 0, 0))
    vec = pl.BlockSpec((1, RWKV_WIDTH), const)
    return pl.pallas_call(
        functools.partial(_rwkv_scan_kernel, chunks),
        grid=(batch, steps),
        in_specs=[tok, tok, mat, mat, tok, tok,
                  pl.BlockSpec((RWKV_WIDTH, RWKV_WIDTH), const), vec, vec],
        out_specs=tok,
        out_shape=jax.ShapeDtypeStruct((n, RWKV_WIDTH), BF16),
        scratch_shapes=[pltpu.VMEM((RWKV_PAIRS, PAIR, PAIR), F32),
                        pltpu.VMEM((tc, RWKV_WIDTH), F32)],
        compiler_params=_cparams("parallel", "arbitrary"),
        name="rwkv_scan",
    )(q1, q2, g, e, bonus, gate, hmean, lnw, lnb)


def _sb_attn_kernel(tq, tk, q_ref, k_ref, v_ref, upper_ref, o_ref):
    qi = pl.program_id(2)
    sub = tq // tk
    neg_upper = upper_ref[...]
    first_row = [(sub - 1 - d) * tk for d in range(sub)]
    causal_of = [lax.broadcasted_iota(jnp.int32, (tq - r0, tk), 1) < lax.broadcasted_iota(jnp.int32, (tq - r0, tk), 0)
                 for r0 in first_row]

    def key_rows(j):
        return pl.ds(pl.multiple_of(j * tk, tk), tk)

    def consume(top, size, carry, diagonal_first):
        tiles = [(h, key_rows(top - d)) + ((first_row[d], causal_of[d]) if diagonal_first and d < sub else (0, None))
                 for d in range(size) for h in range(2)]
        carry = list(carry)
        log_betas, sps16, totals, logits = {}, {}, {}, {}
        for step in range(len(tiles) + 2):
            t = step
            if t < len(tiles):
                h, ks, r0, causal = tiles[t]
                z = _dot(q_ref[r0:, h * PAIR:(h + 1) * PAIR], k_ref[ks, :], NT)
                sp = jnp.maximum(z, 0.0) + jnp.log(1.0 + jnp.exp2(_neg_abs(z))) * LOG2_E
                if causal is not None:
                    sp = jnp.where(causal, sp, 0.0)
                sps16[t] = sp.astype(BF16)
                log_betas[t] = z - sp
                totals[t] = z[:, 0:1]
            t = step - 1
            if 0 <= t < len(tiles):
                logits[t] = log_betas.pop(t) + _dot(sps16.pop(t), neg_upper)
                totals[t] = totals[t] - logits[t][:, 0:1]
            t = step - 2
            if 0 <= t < len(tiles):
                h, ks, r0, causal = tiles[t]
                used, acc = carry[h]
                w = jnp.exp2(logits.pop(t) - used[r0:])
                if causal is not None:
                    w = jnp.where(causal, w, 0.0)
                used_new = used[r0:] + totals.pop(t)
                acc_new = acc[r0:] + _dot(w.astype(BF16), v_ref[ks, :])
                if r0:
                    used_new = jnp.concatenate([used[:r0], used_new], axis=0)
                    acc_new = jnp.concatenate([acc[:r0], acc_new], axis=0)
                carry[h] = (used_new, acc_new)
        return tuple(carry)

    zero = (jnp.zeros((tq, 1), F32), jnp.zeros((tq, PAIR), F32))
    below = qi * sub
    extra = below % KEY_BLOCKS
    top = below + sub - 1

    def first_group(e):
        if e + sub >= KEY_BLOCKS:
            return lambda c: consume(top, sub + e, c, True)
        return lambda c: lax.cond(extra == e, lambda c2: consume(top, sub + e, c2, True), first_group(e + sub), c)

    carry = first_group(0)((zero, zero))
    carry = lax.fori_loop(0, below // KEY_BLOCKS,
                          lambda i, c: consume(below - extra - 1 - KEY_BLOCKS * i, KEY_BLOCKS, c, False), carry)
    lane_lo = lax.broadcasted_iota(jnp.int32, (tq, PAIR), 1) < HEAD_DIM
    o_ref[...] = jnp.where(lane_lo, carry[0][1], carry[1][1]).astype(o_ref.dtype)


def _sb_attn(qm, kn, vb, upper, batch, seq_len, tq, tk):
    n = qm.shape[0]
    steps = seq_len // tq
    assert tq % tk == 0 and KEY_BLOCKS % (tq // tk) == 0
    q_spec = pl.BlockSpec((tq, 2 * PAIR), lambda b, p, i: (b * steps + i, p))
    k_spec = pl.BlockSpec((seq_len, PAIR), lambda b, p, i: (b, p))
    return pl.pallas_call(
        functools.partial(_sb_attn_kernel, tq, tk),
        grid=(batch, SB_PAIRS, steps),
        in_specs=[q_spec, k_spec, k_spec,
                  pl.BlockSpec((tk, tk), lambda b, p, i: (0, 0))],
        out_specs=pl.BlockSpec((tq, PAIR), lambda b, p, i: (b * steps + i, p)),
        out_shape=jax.ShapeDtypeStruct((n, SB_WIDTH), BF16),
        compiler_params=_cparams("parallel", "parallel", "arbitrary"),
        name="sb_attn",
    )(qm, kn, vb, upper)


def _mix_ffn_kernel(blocks_per_seq, tf, x_ref, xh_ref, yr_ref, yrh_ref, ys_ref, ysh_ref, yp_ref, yph_ref,
                    wo_ref, g_ref, wup_ref, cw_ref, cb_ref, wd_ref, o_ref, h_ref, act_ref):
    first = (pl.program_id(0) % blocks_per_seq) == 0
    g = g_ref[...]

    def mixed(x, yr, ys, yp):
        return x + _dot(jnp.concatenate([yr, ys, yp], axis=1), wo_ref[...])

    x = mixed(x_ref[...], yr_ref[...], ys_ref[...], yp_ref[...])
    tail = slice(BF16_ROWS - SUBLANES, BF16_ROWS)
    x_prev = mixed(xh_ref[...], yrh_ref[tail, :], ysh_ref[tail, :], yph_ref[tail, :])
    h_ref[0:SUBLANES, :] = jnp.where(first, 0.0, _rms_rows(x_prev, g)).astype(BF16)
    h_ref[SUBLANES:, :] = _rms_rows(x, g).astype(BF16)
    h = h_ref[...]

    def conv(col):
        cols = slice(col, col + tf)
        up = _dot(h, wup_ref[:, cols])
        cw = cw_ref[:, cols]
        c = cb_ref[:, cols] + up[SUBLANES:] * cw[2:3]
        c = c + pltpu.roll(up, 1, axis=0)[SUBLANES:] * cw[1:2]
        return c + pltpu.roll(up, 2, axis=0)[SUBLANES:] * cw[0:1]

    for j in range(D_FF // tf):
        gate = conv(j * tf)
        val = conv(D_FF + j * tf)
        act_ref[:, j * tf:(j + 1) * tf] = (gate * _sigmoid(gate) * val).astype(BF16)
    o_ref[...] = x + _dot(act_ref[...], wd_ref[...])


def _mix_ffn(x2, yr, ys, yp, w_out_bf16, g, w_up_bf16, conv_w, conv_b, w_down_bf16, seq_len, tm, tf):
    n = x2.shape[0]
    row = lambda i: (i, 0)
    const = lambda i: (0, 0)
    halo = lambda rows: (lambda i: (jnp.maximum(i * (tm // rows) - 1, 0), 0))
    resident = lambda shape: pl.BlockSpec(shape, const, pipeline_mode=pl.Buffered(1))

    def with_halo(width, rows):
        return [pl.BlockSpec((tm, width), row), pl.BlockSpec((rows, width), halo(rows))]

    return pl.pallas_call(
        functools.partial(_mix_ffn_kernel, seq_len // tm, tf),
        grid=(n // tm,),
        in_specs=(with_halo(D_MODEL, SUBLANES) + with_halo(RWKV_WIDTH, BF16_ROWS) + with_halo(SB_WIDTH, BF16_ROWS)
                  + with_halo(POOL_WIDTH, BF16_ROWS)
                  + [resident((D_MODEL, D_MODEL)),
                     pl.BlockSpec((1, D_MODEL), const),
                     resident((D_MODEL, 2 * D_FF)),
                     pl.BlockSpec((3, 2 * D_FF), const),
                     pl.BlockSpec((1, 2 * D_FF), const),
                     resident((D_FF, D_MODEL))]),
        out_specs=pl.BlockSpec((tm, D_MODEL), row),
        out_shape=jax.ShapeDtypeStruct((n, D_MODEL), F32),
        scratch_shapes=[pltpu.VMEM((tm + SUBLANES, D_MODEL), BF16),
                        pltpu.VMEM((tm, D_FF), BF16)],
        compiler_params=_cparams("parallel"),
        name="mix_ffn",
    )(x2, x2, yr, yr, ys, ys, yp, yp, w_out_bf16, g, w_up_bf16, conv_w, conv_b.reshape(1, 2 * D_FF), w_down_bf16)


def _head_indicator(width, value):
    idx = jnp.arange(width) // HEAD_DIM
    return jnp.where(idx[:, None] == idx[None, :], value, 0.0).astype(BF16)


def kernel(x, ln1_g, w_in, mu_shift, w0, w2, a0, a2, g2, k_k, k_a, r_k, lnx_w, lnx_b, v0, v1, v2, q_gain, k_gain, pool_w, pool_b, pool_scale, w_out, ln2_g, w_up, conv_w, conv_b, w_down):
    batch, seq_len, _ = x.shape
    depth = w_in.shape[0]
    n = batch * seq_len
    tm = min(512, seq_len)
    tq = min(512, seq_len)
    tk = min(256, seq_len)
    scan_chunks = min(8, seq_len // CHUNK)
    xform_chunks = min(8, seq_len // CHUNK)
    assert seq_len % tm == 0 and seq_len % (scan_chunks * CHUNK) == 0 and tm % POOL_HALO == 0

    hsum = _head_indicator(RWKV_WIDTH, 1.0)
    hmean = _head_indicator(RWKV_WIDTH, 1.0 / HEAD_DIM)
    tri = jnp.tril(jnp.ones((CHUNK, CHUNK), F32)).astype(BF16)
    upper = (-jnp.tril(jnp.ones((tk, tk), F32), -1)).astype(BF16)
    row = lambda a: a.reshape(1, -1)

    x2 = x.reshape(n, D_MODEL)
    v_first = None
    for l in range(depth):
        rwkv_params = (row(mu_shift[l]), row(w0[l]), w2[l].astype(BF16), row(a0[l]), a2[l].astype(BF16),
                       g2[l].astype(BF16), row(k_k[l]), row(k_a[l]), row(r_k[l]), hsum)
        vmix = None if l == 0 else (v_first, row(v0[l - 1]), v1[l - 1].astype(BF16), v2[l - 1].astype(BF16))
        sb_params = (row(jnp.tile(q_gain[l], PAIR // HEAD_DIM)), row(jnp.tile(k_gain[l], PAIR // HEAD_DIM)),
                     hmean[:PAIR, :PAIR])
        w_bd = jax.scipy.linalg.block_diag(*[pool_w[l, gi] for gi in range(len(POOL_WINDOWS))])
        pool_params = (w_bd.astype(BF16), row(pool_b[l]), row(pool_scale[l]))
        (r, lw, k, v, kk, a, gate, bonus, qm, kn, vb, y_pool) = _front(
            x2, row(ln1_g[l]), w_in[l].astype(BF16), rwkv_params, vmix, sb_params, pool_params, seq_len, tm)
        if l == 0:
            v_first = v
        q1, q2, g_mat, e_mat = _rwkv_chunk(r, lw, k, v, kk, a, tri, xform_chunks)
        y_rwkv = _rwkv_scan(q1, q2, g_mat, e_mat, bonus, gate, hmean, row(lnx_w[l]), row(lnx_b[l]),
                            batch, seq_len, scan_chunks)
        y_sb = _sb_attn(qm, kn, vb, upper, batch, seq_len, tq, tk)

        x2 = _mix_ffn(x2, y_rwkv, y_sb, y_pool, w_out[l].astype(BF16), row(ln2_g[l]), w_up[l].astype(BF16),
                      conv_w[l], conv_b[l], w_down[l].astype(BF16), seq_len, tm, 256)
    return x2.reshape(batch, seq_len, D_MODEL)
```

```python
import functools

import jax
import jax.numpy as jnp
from jax import lax
from jax.experimental import pallas as pl
from jax.experimental.pallas import tpu as pltpu

F32 = jnp.float32
BF16 = jnp.bfloat16

D_MODEL = 1024
HEAD_DIM = 64
RWKV_HEADS = 6
RWKV_WIDTH = RWKV_HEADS * HEAD_DIM
SB_HEADS = 6
SB_WIDTH = SB_HEADS * HEAD_DIM
POOL_WINDOWS = (2, 4, 8, 16)
POOL_WIDTH = D_MODEL - RWKV_WIDTH - SB_WIDTH
POOL_GROUP_DIM = POOL_WIDTH // len(POOL_WINDOWS)
DECAY_LORA = 64
AAA_LORA = 64
GATE_LORA = 128
MV_LORA = 32
RWKV_COLS = 3 * RWKV_WIDTH + DECAY_LORA + AAA_LORA + GATE_LORA
SB_COLS = 3 * SB_WIDTH
IN_COLS = RWKV_COLS + SB_COLS + POOL_WIDTH
D_FF = 2816
NORM_EPS = 1e-6
LN_X_EPS = 64e-5
L2_EPS = 1e-12
LOG2_E = 1.4426950408889634

SUBLANES = 8
BF16_ROWS = 2 * SUBLANES
PAIR = 2 * HEAD_DIM
RWKV_PAIRS = RWKV_WIDTH // PAIR
SB_PAIRS = SB_WIDTH // PAIR
CHUNK = 64
POOL_HALO = 16
VMEM_LIMIT = 48 * 1024 * 1024

TOKEN_BLOCK = 512
FF_TILE = 256
QUERY_BLOCK = 512
KEY_BLOCK = 256
KEY_BLOCKS = 8
SCAN_CHUNKS = 8

NN = (((1,), (0,)), ((), ()))
NT = (((1,), (1,)), ((), ()))
TN = (((0,), (0,)), ((), ()))


def _cparams(*sem):
    return pltpu.CompilerParams(dimension_semantics=sem, vmem_limit_bytes=VMEM_LIMIT)


def _dot(a, b, dims=NN):
    return lax.dot_general(a, b, dims, preferred_element_type=F32)


def _split2(x):
    hi = x.astype(BF16)
    lo = (x - hi.astype(F32)).astype(BF16)
    return hi, lo


def _neg_abs(x):
    return lax.bitcast_convert_type(lax.bitcast_convert_type(x, jnp.int32) | jnp.int32(-2 ** 31), F32)


def _split3(x):
    hi = x.astype(BF16)
    r1 = x - hi.astype(F32)
    mid = r1.astype(BF16)
    lo = (r1 - mid.astype(F32)).astype(BF16)
    return hi, mid, lo


def _head_reduce(a, indicator_bf16):
    return _dot(a.astype(BF16), indicator_bf16)


def _dot_exact_rhs(a, b_bf16, dims=NN):
    h, l = _split2(a)
    return _dot(h, b_bf16, dims) + _dot(l, b_bf16, dims)


def _dot_exact_lhs(a_bf16, b, dims=NN):
    h, m, l = _split3(b)
    return _dot(a_bf16, h, dims) + (_dot(a_bf16, m, dims) + _dot(a_bf16, l, dims))


def _sigmoid(x):
    return 1.0 / (1.0 + jnp.exp(-x))


def _softplus(x):
    return jnp.maximum(x, 0.0) + jnp.log(1.0 + jnp.exp(-jnp.abs(x)))


def _rms_rows(x, g):
    return x * lax.rsqrt(jnp.mean(x * x, axis=-1, keepdims=True) + NORM_EPS) * g


def _front_kernel(blocks_per_seq, has_vmix, *refs):
    refs = list(refs)
    x_ref, xh_ref, ln_ref, win_ref = refs[:4]
    mu_ref, w0_ref, w2_ref, a0_ref, a2_ref, g2_ref, kk_ref, ka_ref, rk_ref, hsum_ref = refs[4:14]
    refs = refs[14:]
    if has_vmix:
        vf_ref, v0_ref, v1_ref, v2_ref = refs[:4]
        refs = refs[4:]
    qg_ref, kg_ref, hmean_ref, pw_ref, pb_ref, ps_ref = refs[:6]
    r_o, lw_o, k_o, v_o, kk_o, a_o, g_o, bonus_o, q_o, kn_o, vb_o, pool_o = refs[6:]

    tm = x_ref.shape[0]
    blk = pl.program_id(0) % blocks_per_seq
    first = blk == 0
    ln = ln_ref[...]
    w_in = win_ref[...]
    proj = _dot(_rms_rows(x_ref[...], ln).astype(BF16), w_in)
    proj_rest = proj[:, RWKV_COLS:]
    proj_prev = jnp.where(first, 0.0, _dot(_rms_rows(xh_ref[...], ln).astype(BF16), w_in))

    p = proj[:, :RWKV_COLS]
    prev_last = proj_prev[POOL_HALO - 1:POOL_HALO, :RWKV_COLS]
    row = lax.broadcasted_iota(jnp.int32, (tm, 1), 0)
    shifted = jnp.where(row == 0, prev_last, pltpu.roll(p, 1, axis=0))
    p = p + (shifted - p) * mu_ref[...]

    c0, c1, c2 = RWKV_WIDTH, 2 * RWKV_WIDTH, 3 * RWKV_WIDTH
    r = p[:, 0:c0]
    k = p[:, c0:c1]
    v = p[:, c1:c2]
    xwa = p[:, c2:c2 + DECAY_LORA + AAA_LORA]
    xw = xwa[:, :DECAY_LORA]
    xa = xwa[:, DECAY_LORA:]
    xg = p[:, c2 + DECAY_LORA + AAA_LORA:]

    lora = lambda t, w_ref: _dot(t.astype(BF16), w_ref[...])
    w = -_softplus(-(w0_ref[...] + lora(jnp.tanh(xw), w2_ref))) - 0.5
    lw_o[...] = -jnp.exp(w)
    a = _sigmoid(a0_ref[...] + lora(xa, a2_ref))
    g_o[...] = lora(_sigmoid(xg), g2_ref)
    if has_vmix:
        mix = _sigmoid(v0_ref[...] + lora(lora(v, v1_ref), v2_ref))
        v = v + (vf_ref[...] - v) * mix
    hsum = hsum_ref[...]
    kk = k * kk_ref[...]
    kk = kk / jnp.maximum(jnp.sqrt(_head_reduce(kk * kk, hsum)), L2_EPS)
    k = k * (1.0 + (a - 1.0) * ka_ref[...])
    r_o[...] = r
    k_o[...] = k
    v_o[...] = v
    kk_o[...] = kk
    a_o[...] = a
    bonus_o[...] = _head_reduce(r * k * rk_ref[...], hsum) * v

    hmean = hmean_ref[...]
    lane_lo = lax.broadcasted_iota(jnp.int32, (tm, PAIR), 1) < HEAD_DIM
    for pair in range(SB_PAIRS):
        qc = pair * PAIR
        q = proj_rest[:, qc:qc + PAIR]
        kx = proj_rest[:, qc + SB_WIDTH:qc + SB_WIDTH + PAIR]
        qn = q * lax.rsqrt(_head_reduce(q * q, hmean) + NORM_EPS) * qg_ref[...] * (LOG2_E * HEAD_DIM ** -0.5)
        kn = kx * lax.rsqrt(_head_reduce(kx * kx, hmean) + NORM_EPS) * kg_ref[...]
        q_o[:, 2 * pair * PAIR:(2 * pair + 1) * PAIR] = jnp.where(lane_lo, qn, 0.0).astype(BF16)
        q_o[:, (2 * pair + 1) * PAIR:(2 * pair + 2) * PAIR] = jnp.where(lane_lo, 0.0, qn).astype(BF16)
        kn_o[:, pair * PAIR:(pair + 1) * PAIR] = kn.astype(BF16)
    vb_o[...] = proj_rest[:, 2 * SB_WIDTH:SB_COLS].astype(BF16)

    u = proj_rest[:, SB_COLS:]
    ext = jnp.concatenate([proj_prev[:, RWKV_COLS + SB_COLS:], u], axis=0)
    pos = (blk * tm + row).astype(F32)
    group = lax.broadcasted_iota(jnp.int32, (tm, POOL_WIDTH), 1) // POOL_GROUP_DIM
    acc = ext
    span = 1
    pooled = jnp.zeros((tm, POOL_WIDTH), F32)
    for gi, win in enumerate(POOL_WINDOWS):
        while span < win:
            acc = acc + pltpu.roll(acc, span, axis=0)
            span *= 2
        count = jnp.minimum(pos + 1.0, float(win))
        pooled = jnp.where(group == gi, acc[POOL_HALO:] / count, pooled)
    pooled = pooled - u
    pool_o[...] = ((_dot(pooled.astype(BF16), pw_ref[...]) + pb_ref[...]) * ps_ref[...]).astype(BF16)


def _front(x2, ln, w_in_bf16, rwkv_params, vmix, sb_params, pool_params, seq_len, tm):
    n = x2.shape[0]
    has_vmix = vmix is not None
    row = lambda i: (i, 0)
    const = lambda i: (0, 0)
    halo_idx = lambda i: (jnp.maximum(i * (tm // POOL_HALO) - 1, 0), 0)
    full = lambda a: pl.BlockSpec(a.shape, const)
    in_specs = [pl.BlockSpec((tm, D_MODEL), row),
                pl.BlockSpec((POOL_HALO, D_MODEL), halo_idx),
                full(ln),
                pl.BlockSpec((D_MODEL, IN_COLS), const, pipeline_mode=pl.Buffered(1))]
    in_specs += [full(a) for a in rwkv_params]
    args = [x2, x2, ln, w_in_bf16] + list(rwkv_params)
    if has_vmix:
        in_specs += [pl.BlockSpec((tm, RWKV_WIDTH), row)] + [full(a) for a in vmix[1:]]
        args += list(vmix)
    in_specs += [full(a) for a in sb_params + pool_params]
    args += list(sb_params + pool_params)
    tok = lambda width: pl.BlockSpec((tm, width), row)
    shape = lambda width, dtype: jax.ShapeDtypeStruct((n, width), dtype)
    return pl.pallas_call(
        functools.partial(_front_kernel, seq_len // tm, has_vmix),
        grid=(n // tm,),
        in_specs=in_specs,
        out_specs=[tok(RWKV_WIDTH)] * 8 + [tok(2 * SB_WIDTH), tok(SB_WIDTH), tok(SB_WIDTH), tok(POOL_WIDTH)],
        out_shape=[shape(RWKV_WIDTH, F32)] * 8 + [shape(2 * SB_WIDTH, BF16), shape(SB_WIDTH, BF16),
                                                  shape(SB_WIDTH, BF16), shape(POOL_WIDTH, BF16)],
        compiler_params=_cparams("parallel"),
        name="front",
    )(*args)


def _pair_rows(x, lane_lo):
    return jnp.concatenate([jnp.where(lane_lo, x, 0.0), jnp.where(lane_lo, 0.0, x)], axis=0)


def _rwkv_chunk_kernel(chunks, r_ref, lw_ref, k_ref, v_ref, kk_ref, a_ref, tri_ref,
                       q1_o, q2_o, g_o, e_o):
    two = 2 * CHUNK
    ri = lax.broadcasted_iota(jnp.int32, (two, two), 0)
    ci = lax.broadcasted_iota(jnp.int32, (two, two), 1)
    same = (ri // CHUNK) == (ci // CHUNK)
    strict = same & ((ci % CHUNK) < (ri % CHUNK))
    incl = same & ((ci % CHUNK) <= (ri % CHUNK))
    eye = (ri == ci).astype(F32)
    lane_lo = lax.broadcasted_iota(jnp.int32, (CHUNK, PAIR), 1) < HEAD_DIM
    tri = tri_ref[...]

    units = []
    for c in range(chunks):
        rows = pl.ds(c * CHUNK, CHUNK)
        lw = lw_ref[rows, :]
        cl = _dot_exact_lhs(tri, lw)
        e_pos = jnp.exp(cl)
        e_prev = jnp.exp(cl - lw)
        e_neg = jnp.exp(-cl)
        kk = kk_ref[rows, :]
        at = -kk * e_prev
        bt = kk * a_ref[rows, :] * e_neg
        rt = r_ref[rows, :] * e_pos
        kt = k_ref[rows, :] * e_neg
        v = v_ref[rows, :]
        wc = e_pos[CHUNK - 1:CHUNK, :]
        for p in range(RWKV_PAIRS):
            ls = slice(p * PAIR, (p + 1) * PAIR)
            units.append(dict(
                c=c, p=p, rows=rows, ls=ls, wc=wc[:, ls],
                xa=_pair_rows(at[:, ls], lane_lo).astype(BF16), xr=_pair_rows(rt[:, ls], lane_lo),
                vv=_pair_rows(v[:, ls], lane_lo).astype(BF16), v=v[:, ls].astype(BF16),
                bt=bt[:, ls].astype(BF16), kt=kt[:, ls].astype(BF16)))

    for u in units:
        sc = _dot(jnp.concatenate([u["xa"], u["xr"].astype(BF16)], axis=0),
                  jnp.concatenate([u["bt"], u["bt"], u["kt"], u["kt"]], axis=0), NT)
        a_ab = jnp.where(strict, sc[:two, :two], 0.0)
        u["a_ak"] = jnp.where(strict, sc[:two, two:], 0.0).astype(BF16)
        u["m"] = jnp.concatenate([jnp.where(incl, sc[two:, :two], 0.0),
                                  jnp.where(incl, sc[two:, two:], 0.0)], axis=1).astype(BF16)
        u["t"] = eye + a_ab
        u["pw"] = a_ab.astype(BF16)

    for _ in range(CHUNK.bit_length() - 2):
        for u in units:
            u["pw"] = _dot(u["pw"], u["pw"]).astype(BF16)
        for u in units:
            u["t"] = u["t"] + _dot(u["t"].astype(BF16), u["pw"])

    for u in units:
        u["z"] = _dot(u["a_ak"], u["vv"]).astype(BF16)
    for u in units:
        u["pm"] = _dot(u["t"].astype(BF16), jnp.concatenate([u["xa"], u["z"]], axis=1))
    for u in units:
        vv = u["vv"]
        qm = _dot(u["m"], jnp.concatenate([u["pm"].astype(BF16),
                                           jnp.concatenate([jnp.zeros_like(vv), vv], axis=1)], axis=0))
        q1m = u["xr"] + qm[:, :PAIR]
        q2m = qm[:, PAIR:]
        q1_o[u["rows"], u["ls"]] = (q1m[:CHUNK] + q1m[CHUNK:]).astype(BF16)
        q2_o[u["rows"], u["ls"]] = q2m[:CHUNK] + q2m[CHUNK:]
    for u in units:
        pm = u["pm"]
        p12 = (pm[:CHUNK] + pm[CHUNK:]).astype(BF16)
        lt = jnp.concatenate([p12, jnp.concatenate([jnp.zeros_like(u["bt"]), u["v"]], axis=1)], axis=0)
        ge = _dot(lt, jnp.concatenate([u["bt"], u["kt"]], axis=0), TN)
        g_o[u["c"], u["p"]] = jnp.where(same, (eye + ge[:PAIR]) * u["wc"], 0.0).astype(BF16)
        e_o[u["c"], u["p"]] = jnp.where(same, ge[PAIR:] * u["wc"], 0.0)


def _rwkv_chunk(r, lw, k, v, kk, a, tri, chunks):
    n = r.shape[0]
    tc = chunks * CHUNK
    row = lambda i: (i, 0)
    tok = pl.BlockSpec((tc, RWKV_WIDTH), row)
    mat = pl.BlockSpec((chunks, RWKV_PAIRS, PAIR, PAIR), lambda i: (i, 0, 0, 0))
    tok_shape = lambda dtype: jax.ShapeDtypeStruct((n, RWKV_WIDTH), dtype)
    mat_shape = lambda dtype: jax.ShapeDtypeStruct((n // CHUNK, RWKV_PAIRS, PAIR, PAIR), dtype)
    return pl.pallas_call(
        functools.partial(_rwkv_chunk_kernel, chunks),
        grid=(n // tc,),
        in_specs=[tok] * 6 + [pl.BlockSpec((CHUNK, CHUNK), lambda i: (0, 0))],
        out_specs=[tok, tok, mat, mat],
        out_shape=[tok_shape(BF16), tok_shape(F32), mat_shape(BF16), mat_shape(F32)],
        compiler_params=_cparams("parallel"),
        name="rwkv_chunk",
    )(r, lw, k, v, kk, a, tri)


def _rwkv_scan_kernel(chunks, seqs, q1_ref, q2_ref, g_ref, e_ref, bonus_ref, gate_ref, hmean_ref,
                      lnw_ref, lnb_ref, o_ref, s_ref, y_ref):
    @pl.when(pl.program_id(1) == 0)
    def _():
        s_ref[...] = jnp.zeros_like(s_ref)

    for c in range(chunks):
        rows = pl.ds(c * CHUNK, CHUNK)
        for b in range(seqs):
            for p in range(RWKV_PAIRS):
                ls = slice(p * PAIR, (p + 1) * PAIR)
                s = s_ref[b, p]
                s_hi, s_lo = _split2(s)
                y_ref[b, rows, ls] = _dot(q1_ref[b, rows, ls], s_hi, NT) + q2_ref[b, rows, ls]
                g = g_ref[b, c, p]
                s_ref[b, p] = _dot(s_hi, g) + (_dot(s_lo, g) + e_ref[b, c, p])

    hmean = hmean_ref[...]
    for b in range(seqs):
        y = y_ref[b]
        mean = _dot_exact_rhs(y, hmean)
        yc = y - mean
        var = _head_reduce(yc * yc, hmean)
        yn = yc * lax.rsqrt(var + LN_X_EPS) * lnw_ref[...] + lnb_ref[...]
        o_ref[b] = ((yn + bonus_ref[b]) * gate_ref[b]).astype(o_ref.dtype)


def _rwkv_scan(q1, q2, g, e, bonus, gate, hmean, lnw, lnb, batch, seq_len, chunks):
    n = q1.shape[0]
    tc = chunks * CHUNK
    seqs = next(s for s in (4, 2, 1) if batch % s == 0)
    const = lambda b, i: (0, 0)
    tok = pl.BlockSpec((seqs, tc, RWKV_WIDTH), lambda b, i: (b, i, 0))
    mat = pl.BlockSpec((seqs, chunks, RWKV_PAIRS, PAIR, PAIR), lambda b, i: (b, i, 0, 0, 0))
    vec = pl.BlockSpec((1, RWKV_WIDTH), const)
    per_seq = lambda a: a.reshape((batch, a.shape[0] // batch) + a.shape[1:])
    out = pl.pallas_call(
        functools.partial(_rwkv_scan_kernel, chunks, seqs),
        grid=(batch // seqs, seq_len // tc),
        in_specs=[tok, tok, mat, mat, tok, tok,
                  pl.BlockSpec((RWKV_WIDTH, RWKV_WIDTH), const), vec, vec],
        out_specs=tok,
        out_shape=jax.ShapeDtypeStruct((batch, seq_len, RWKV_WIDTH), BF16),
        scratch_shapes=[pltpu.VMEM((seqs, RWKV_PAIRS, PAIR, PAIR), F32),
                        pltpu.VMEM((seqs, tc, RWKV_WIDTH), F32)],
        compiler_params=_cparams("parallel", "arbitrary"),
        name="rwkv_scan",
    )(per_seq(q1), per_seq(q2), per_seq(g), per_seq(e), per_seq(bonus), per_seq(gate), hmean, lnw, lnb)
    return out.reshape(n, RWKV_WIDTH)


def _sb_attn_kernel(tq, tk, q_ref, k_ref, v_ref, upper_ref, o_ref):
    qi = pl.program_id(2)
    sub = tq // tk
    neg_upper = upper_ref[...]
    first_row = [(sub - 1 - d) * tk for d in range(sub)]
    causal_of = [lax.broadcasted_iota(jnp.int32, (tq - r0, tk), 1) < lax.broadcasted_iota(jnp.int32, (tq - r0, tk), 0)
                 for r0 in first_row]

    def key_rows(j):
        return pl.ds(pl.multiple_of(j * tk, tk), tk)

    def consume(top, size, carry, diagonal_first):
        tiles = [(h, key_rows(top - d)) + ((first_row[d], causal_of[d]) if diagonal_first and d < sub else (0, None))
                 for d in range(size) for h in range(2)]
        carry = list(carry)
        log_betas, sps16, totals, logits = {}, {}, {}, {}
        for step in range(len(tiles) + 2):
            t = step
            if t < len(tiles):
                h, ks, r0, causal = tiles[t]
                z = _dot(q_ref[r0:, h * PAIR:(h + 1) * PAIR], k_ref[ks, :], NT)
                sp = jnp.maximum(z, 0.0) + jnp.log(1.0 + jnp.exp2(_neg_abs(z))) * LOG2_E
                if causal is not None:
                    sp = jnp.where(causal, sp, 0.0)
                sps16[t] = sp.astype(BF16)
                log_betas[t] = z - sp
                totals[t] = z[:, 0:1]
            t = step - 1
            if 0 <= t < len(tiles):
                logits[t] = log_betas.pop(t) + _dot(sps16.pop(t), neg_upper)
                totals[t] = totals[t] - logits[t][:, 0:1]
            t = step - 2
            if 0 <= t < len(tiles):
                h, ks, r0, causal = tiles[t]
                used, acc = carry[h]
                w = jnp.exp2(logits.pop(t) - used[r0:])
                if causal is not None:
                    w = jnp.where(causal, w, 0.0)
                used_new = used[r0:] + totals.pop(t)
                acc_new = acc[r0:] + _dot(w.astype(BF16), v_ref[ks, :])
                if r0:
                    used_new = jnp.concatenate([used[:r0], used_new], axis=0)
                    acc_new = jnp.concatenate([acc[:r0], acc_new], axis=0)
                carry[h] = (used_new, acc_new)
        return tuple(carry)

    zero = (jnp.zeros((tq, 1), F32), jnp.zeros((tq, PAIR), F32))
    below = qi * sub
    extra = below % KEY_BLOCKS
    top = below + sub - 1

    def first_group(e):
        if e + sub >= KEY_BLOCKS:
            return lambda c: consume(top, sub + e, c, True)
        return lambda c: lax.cond(extra == e, lambda c2: consume(top, sub + e, c2, True), first_group(e + sub), c)

    carry = first_group(0)((zero, zero))
    carry = lax.fori_loop(0, below // KEY_BLOCKS,
                          lambda i, c: consume(below - extra - 1 - KEY_BLOCKS * i, KEY_BLOCKS, c, False), carry)
    lane_lo = lax.broadcasted_iota(jnp.int32, (tq, PAIR), 1) < HEAD_DIM
    o_ref[...] = jnp.where(lane_lo, carry[0][1], carry[1][1]).astype(o_ref.dtype)


def _sb_attn(qm, kn, vb, upper, batch, seq_len, tq, tk):
    n = qm.shape[0]
    steps = seq_len // tq
    assert tq % tk == 0 and KEY_BLOCKS % (tq // tk) == 0
    q_spec = pl.BlockSpec((tq, 2 * PAIR), lambda b, p, i: (b * steps + i, p))
    k_spec = pl.BlockSpec((seq_len, PAIR), lambda b, p, i: (b, p))
    return pl.pallas_call(
        functools.partial(_sb_attn_kernel, tq, tk),
        grid=(batch, SB_PAIRS, steps),
        in_specs=[q_spec, k_spec, k_spec,
                  pl.BlockSpec((tk, tk), lambda b, p, i: (0, 0))],
        out_specs=pl.BlockSpec((tq, PAIR), lambda b, p, i: (b * steps + i, p)),
        out_shape=jax.ShapeDtypeStruct((n, SB_WIDTH), BF16),
        compiler_params=_cparams("parallel", "parallel", "arbitrary"),
        name="sb_attn",
    )(qm, kn, vb, upper)


def _mix_ffn_kernel(blocks_per_seq, tf, x_ref, xh_ref, yr_ref, yrh_ref, ys_ref, ysh_ref, yp_ref, yph_ref,
                    wo_ref, g_ref, wup_ref, cw_ref, cb_ref, wd_ref, o_ref, h_ref, act_ref):
    first = (pl.program_id(0) % blocks_per_seq) == 0
    g = g_ref[...]

    def mixed(x, yr, ys, yp):
        return x + _dot(jnp.concatenate([yr, ys, yp], axis=1), wo_ref[...])

    x = mixed(x_ref[...], yr_ref[...], ys_ref[...], yp_ref[...])
    tail = slice(BF16_ROWS - SUBLANES, BF16_ROWS)
    x_prev = mixed(xh_ref[...], yrh_ref[tail, :], ysh_ref[tail, :], yph_ref[tail, :])
    h_ref[0:SUBLANES, :] = jnp.where(first, 0.0, _rms_rows(x_prev, g)).astype(BF16)
    h_ref[SUBLANES:, :] = _rms_rows(x, g).astype(BF16)
    h = h_ref[...]

    def conv(col):
        cols = slice(col, col + tf)
        up = _dot(h, wup_ref[:, cols])
        cw = cw_ref[:, cols]
        c = cb_ref[:, cols] + up[SUBLANES:] * cw[2:3]
        c = c + pltpu.roll(up, 1, axis=0)[SUBLANES:] * cw[1:2]
        return c + pltpu.roll(up, 2, axis=0)[SUBLANES:] * cw[0:1]

    for j in range(D_FF // tf):
        gate = conv(j * tf)
        val = conv(D_FF + j * tf)
        act_ref[:, j * tf:(j + 1) * tf] = (gate * _sigmoid(gate) * val).astype(BF16)
    o_ref[...] = x + _dot(act_ref[...], wd_ref[...])


def _mix_ffn(x2, yr, ys, yp, w_out_bf16, g, w_up_bf16, conv_w, conv_b, w_down_bf16, seq_len, tm, tf):
    n = x2.shape[0]
    row = lambda i: (i, 0)
    const = lambda i: (0, 0)
    halo = lambda rows: (lambda i: (jnp.maximum(i * (tm // rows) - 1, 0), 0))
    resident = lambda shape: pl.BlockSpec(shape, const, pipeline_mode=pl.Buffered(1))

    def with_halo(width, rows):
        return [pl.BlockSpec((tm, width), row), pl.BlockSpec((rows, width), halo(rows))]

    return pl.pallas_call(
        functools.partial(_mix_ffn_kernel, seq_len // tm, tf),
        grid=(n // tm,),
        in_specs=(with_halo(D_MODEL, SUBLANES) + with_halo(RWKV_WIDTH, BF16_ROWS) + with_halo(SB_WIDTH, BF16_ROWS)
                  + with_halo(POOL_WIDTH, BF16_ROWS)
                  + [resident((D_MODEL, D_MODEL)),
                     pl.BlockSpec((1, D_MODEL), const),
                     resident((D_MODEL, 2 * D_FF)),
                     pl.BlockSpec((3, 2 * D_FF), const),
                     pl.BlockSpec((1, 2 * D_FF), const),
                     resident((D_FF, D_MODEL))]),
        out_specs=pl.BlockSpec((tm, D_MODEL), row),
        out_shape=jax.ShapeDtypeStruct((n, D_MODEL), F32),
        scratch_shapes=[pltpu.VMEM((tm + SUBLANES, D_MODEL), BF16),
                        pltpu.VMEM((tm, D_FF), BF16)],
        compiler_params=_cparams("parallel"),
        name="mix_ffn",
    )(x2, x2, yr, yr, ys, ys, yp, yp, w_out_bf16, g, w_up_bf16, conv_w, conv_b.reshape(1, 2 * D_FF), w_down_bf16)


def _head_indicator(width, value):
    idx = jnp.arange(width) // HEAD_DIM
    return jnp.where(idx[:, None] == idx[None, :], value, 0.0).astype(BF16)


def kernel(x, ln1_g, w_in, mu_shift, w0, w2, a0, a2, g2, k_k, k_a, r_k, lnx_w, lnx_b, v0, v1, v2, q_gain, k_gain, pool_w, pool_b, pool_scale, w_out, ln2_g, w_up, conv_w, conv_b, w_down):
    batch, seq_len, _ = x.shape
    depth = w_in.shape[0]
    n = batch * seq_len
    tm = min(TOKEN_BLOCK, seq_len)
    tq = min(QUERY_BLOCK, seq_len)
    tk = min(KEY_BLOCK, seq_len)
    scan_chunks = min(SCAN_CHUNKS, seq_len // CHUNK)
    assert seq_len % tm == 0 and seq_len % tq == 0 and seq_len % (scan_chunks * CHUNK) == 0
    assert tm % POOL_HALO == 0 and tm % BF16_ROWS == 0 and D_FF % FF_TILE == 0

    hsum = _head_indicator(RWKV_WIDTH, 1.0)
    hmean = _head_indicator(RWKV_WIDTH, 1.0 / HEAD_DIM)
    tri = jnp.tril(jnp.ones((CHUNK, CHUNK), F32)).astype(BF16)
    upper = (-jnp.tril(jnp.ones((tk, tk), F32), -1)).astype(BF16)
    row = lambda a: a.reshape(1, -1)

    x2 = x.reshape(n, D_MODEL)
    v_first = None
    for l in range(depth):
        rwkv_params = (row(mu_shift[l]), row(w0[l]), w2[l].astype(BF16), row(a0[l]), a2[l].astype(BF16),
                       g2[l].astype(BF16), row(k_k[l]), row(k_a[l]), row(r_k[l]), hsum)
        vmix = None if l == 0 else (v_first, row(v0[l - 1]), v1[l - 1].astype(BF16), v2[l - 1].astype(BF16))
        sb_params = (row(jnp.tile(q_gain[l], PAIR // HEAD_DIM)), row(jnp.tile(k_gain[l], PAIR // HEAD_DIM)),
                     hmean[:PAIR, :PAIR])
        w_bd = jax.scipy.linalg.block_diag(*[pool_w[l, gi] for gi in range(len(POOL_WINDOWS))])
        pool_params = (w_bd.astype(BF16), row(pool_b[l]), row(pool_scale[l]))
        (r, lw, k, v, kk, a, gate, bonus, qm, kn, vb, y_pool) = _front(
            x2, row(ln1_g[l]), w_in[l].astype(BF16), rwkv_params, vmix, sb_params, pool_params, seq_len, tm)
        if l == 0:
            v_first = v
        q1, q2, g_mat, e_mat = _rwkv_chunk(r, lw, k, v, kk, a, tri, scan_chunks)
        y_rwkv = _rwkv_scan(q1, q2, g_mat, e_mat, bonus, gate, hmean, row(lnx_w[l]), row(lnx_b[l]),
                            batch, seq_len, scan_chunks)
        y_sb = _sb_attn(qm, kn, vb, upper, batch, seq_len, tq, tk)

        x2 = _mix_ffn(x2, y_rwkv, y_sb, y_pool, w_out[l].astype(BF16), row(ln2_g[l]), w_up[l].astype(BF16),
                      conv_w[l], conv_b[l], w_down[l].astype(BF16), seq_len, tm, FF_TILE)
    return x2.reshape(batch, seq_len, D_MODEL)
```

```python
import functools

import jax
import jax.numpy as jnp
from jax import lax
from jax.experimental import pallas as pl
from jax.experimental.pallas import tpu as pltpu

F32 = jnp.float32
BF16 = jnp.bfloat16

D_MODEL = 1024
HEAD_DIM = 64
RWKV_HEADS = 6
RWKV_WIDTH = RWKV_HEADS * HEAD_DIM
SB_HEADS = 6
SB_WIDTH = SB_HEADS * HEAD_DIM
POOL_WINDOWS = (2, 4, 8, 16)
POOL_WIDTH = D_MODEL - RWKV_WIDTH - SB_WIDTH
POOL_GROUP_DIM = POOL_WIDTH // len(POOL_WINDOWS)
DECAY_LORA = 64
AAA_LORA = 64
GATE_LORA = 128
MV_LORA = 32
RWKV_COLS = 3 * RWKV_WIDTH + DECAY_LORA + AAA_LORA + GATE_LORA
SB_COLS = 3 * SB_WIDTH
IN_COLS = RWKV_COLS + SB_COLS + POOL_WIDTH
D_FF = 2816
NORM_EPS = 1e-6
LN_X_EPS = 64e-5
L2_EPS = 1e-12
LOG2_E = 1.4426950408889634

SUBLANES = 8
BF16_ROWS = 2 * SUBLANES
PAIR = 2 * HEAD_DIM
RWKV_PAIRS = RWKV_WIDTH // PAIR
SB_PAIRS = SB_WIDTH // PAIR
CHUNK = 64
POOL_HALO = 16
VMEM_LIMIT = 48 * 1024 * 1024

TOKEN_BLOCK = 512
FF_TILE = 256
QUERY_BLOCK = 512
KEY_BLOCK = 256
KEY_BLOCKS = 8
SCAN_CHUNKS = 8

NN = (((1,), (0,)), ((), ()))
NT = (((1,), (1,)), ((), ()))
TN = (((0,), (0,)), ((), ()))


def _cparams(*sem):
    return pltpu.CompilerParams(dimension_semantics=sem, vmem_limit_bytes=VMEM_LIMIT)


def _dot(a, b, dims=NN):
    return lax.dot_general(a, b, dims, preferred_element_type=F32)


def _split2(x):
    hi = x.astype(BF16)
    lo = (x - hi.astype(F32)).astype(BF16)
    return hi, lo


def _neg_abs(x):
    return lax.bitcast_convert_type(lax.bitcast_convert_type(x, jnp.int32) | jnp.int32(-2 ** 31), F32)


def _split3(x):
    hi = x.astype(BF16)
    r1 = x - hi.astype(F32)
    mid = r1.astype(BF16)
    lo = (r1 - mid.astype(F32)).astype(BF16)
    return hi, mid, lo


def _head_reduce(a, indicator_bf16):
    return _dot(a.astype(BF16), indicator_bf16)


def _dot_exact_rhs(a, b_bf16, dims=NN):
    h, l = _split2(a)
    return _dot(h, b_bf16, dims) + _dot(l, b_bf16, dims)


def _dot_exact_lhs(a_bf16, b, dims=NN):
    h, m, l = _split3(b)
    return _dot(a_bf16, h, dims) + (_dot(a_bf16, m, dims) + _dot(a_bf16, l, dims))


def _sigmoid(x):
    return 1.0 / (1.0 + jnp.exp(-x))


def _softplus(x):
    return jnp.maximum(x, 0.0) + jnp.log(1.0 + jnp.exp(-jnp.abs(x)))


def _rms_rows(x, g):
    return x * lax.rsqrt(jnp.mean(x * x, axis=-1, keepdims=True) + NORM_EPS) * g


def _front_kernel(blocks_per_seq, has_vmix, *refs):
    refs = list(refs)
    x_ref, xh_ref, ln_ref, win_ref = refs[:4]
    mu_ref, w0_ref, w2_ref, a0_ref, a2_ref, g2_ref, kk_ref, ka_ref, rk_ref, hsum_ref = refs[4:14]
    refs = refs[14:]
    if has_vmix:
        vf_ref, v0_ref, v1_ref, v2_ref = refs[:4]
        refs = refs[4:]
    qg_ref, kg_ref, hmean_ref, pw_ref, pb_ref, ps_ref = refs[:6]
    r_o, lw_o, k_o, v_o, kk_o, a_o, g_o, bonus_o, q_o, kn_o, vb_o, pool_o = refs[6:]

    tm = x_ref.shape[0]
    blk = pl.program_id(0) % blocks_per_seq
    first = blk == 0
    ln = ln_ref[...]
    w_in = win_ref[...]
    proj = _dot(_rms_rows(x_ref[...], ln).astype(BF16), w_in)
    proj_rest = proj[:, RWKV_COLS:]
    proj_prev = jnp.where(first, 0.0, _dot(_rms_rows(xh_ref[...], ln).astype(BF16), w_in))

    p = proj[:, :RWKV_COLS]
    prev_last = proj_prev[POOL_HALO - 1:POOL_HALO, :RWKV_COLS]
    row = lax.broadcasted_iota(jnp.int32, (tm, 1), 0)
    shifted = jnp.where(row == 0, prev_last, pltpu.roll(p, 1, axis=0))
    p = p + (shifted - p) * mu_ref[...]

    c0, c1, c2 = RWKV_WIDTH, 2 * RWKV_WIDTH, 3 * RWKV_WIDTH
    r = p[:, 0:c0]
    k = p[:, c0:c1]
    v = p[:, c1:c2]
    xwa = p[:, c2:c2 + DECAY_LORA + AAA_LORA]
    xw = xwa[:, :DECAY_LORA]
    xa = xwa[:, DECAY_LORA:]
    xg = p[:, c2 + DECAY_LORA + AAA_LORA:]

    lora = lambda t, w_ref: _dot(t.astype(BF16), w_ref[...])
    w = -_softplus(-(w0_ref[...] + lora(jnp.tanh(xw), w2_ref))) - 0.5
    lw_o[...] = -jnp.exp(w)
    a = _sigmoid(a0_ref[...] + lora(xa, a2_ref))
    g_o[...] = lora(_sigmoid(xg), g2_ref)
    if has_vmix:
        mix = _sigmoid(v0_ref[...] + lora(lora(v, v1_ref), v2_ref))
        v = v + (vf_ref[...] - v) * mix
    hsum = hsum_ref[...]
    kk = k * kk_ref[...]
    kk = kk / jnp.maximum(jnp.sqrt(_head_reduce(kk * kk, hsum)), L2_EPS)
    k = k * (1.0 + (a - 1.0) * ka_ref[...])
    r_o[...] = r
    k_o[...] = k
    v_o[...] = v
    kk_o[...] = kk
    a_o[...] = a
    bonus_o[...] = _head_reduce(r * k * rk_ref[...], hsum) * v

    hmean = hmean_ref[...]
    lane_lo = lax.broadcasted_iota(jnp.int32, (tm, PAIR), 1) < HEAD_DIM
    for pair in range(SB_PAIRS):
        qc = pair * PAIR
        q = proj_rest[:, qc:qc + PAIR]
        kx = proj_rest[:, qc + SB_WIDTH:qc + SB_WIDTH + PAIR]
        qn = q * lax.rsqrt(_head_reduce(q * q, hmean) + NORM_EPS) * qg_ref[...] * (LOG2_E * HEAD_DIM ** -0.5)
        kn = kx * lax.rsqrt(_head_reduce(kx * kx, hmean) + NORM_EPS) * kg_ref[...]
        q_o[:, 2 * pair * PAIR:(2 * pair + 1) * PAIR] = jnp.where(lane_lo, qn, 0.0).astype(BF16)
        q_o[:, (2 * pair + 1) * PAIR:(2 * pair + 2) * PAIR] = jnp.where(lane_lo, 0.0, qn).astype(BF16)
        kn_o[:, pair * PAIR:(pair + 1) * PAIR] = kn.astype(BF16)
    vb_o[...] = proj_rest[:, 2 * SB_WIDTH:SB_COLS].astype(BF16)

    u = proj_rest[:, SB_COLS:]
    ext = jnp.concatenate([proj_prev[:, RWKV_COLS + SB_COLS:], u], axis=0)
    pos = (blk * tm + row).astype(F32)
    group = lax.broadcasted_iota(jnp.int32, (tm, POOL_WIDTH), 1) // POOL_GROUP_DIM
    acc = ext
    span = 1
    pooled = jnp.zeros((tm, POOL_WIDTH), F32)
    for gi, win in enumerate(POOL_WINDOWS):
        while span < win:
            acc = acc + pltpu.roll(acc, span, axis=0)
            span *= 2
        count = jnp.minimum(pos + 1.0, float(win))
        pooled = jnp.where(group == gi, acc[POOL_HALO:] / count, pooled)
    pooled = pooled - u
    pool_o[...] = ((_dot(pooled.astype(BF16), pw_ref[...]) + pb_ref[...]) * ps_ref[...]).astype(BF16)


def _layer_weight(stacked, layer):
    return pl.BlockSpec((None,) + stacked.shape[1:], lambda i: (layer, 0, 0), pipeline_mode=pl.Buffered(1))


def _front(x2, ln, w_in_bf16, layer, rwkv_params, vmix, sb_params, pool_params, seq_len, tm):
    n = x2.shape[0]
    has_vmix = vmix is not None
    row = lambda i: (i, 0)
    const = lambda i: (0, 0)
    halo_idx = lambda i: (jnp.maximum(i * (tm // POOL_HALO) - 1, 0), 0)
    full = lambda a: pl.BlockSpec(a.shape, const)
    in_specs = [pl.BlockSpec((tm, D_MODEL), row),
                pl.BlockSpec((POOL_HALO, D_MODEL), halo_idx),
                full(ln),
                _layer_weight(w_in_bf16, layer)]
    in_specs += [full(a) for a in rwkv_params]
    args = [x2, x2, ln, w_in_bf16] + list(rwkv_params)
    if has_vmix:
        in_specs += [pl.BlockSpec((tm, RWKV_WIDTH), row)] + [full(a) for a in vmix[1:]]
        args += list(vmix)
    in_specs += [full(a) for a in sb_params + pool_params]
    args += list(sb_params + pool_params)
    tok = lambda width: pl.BlockSpec((tm, width), row)
    shape = lambda width, dtype: jax.ShapeDtypeStruct((n, width), dtype)
    return pl.pallas_call(
        functools.partial(_front_kernel, seq_len // tm, has_vmix),
        grid=(n // tm,),
        in_specs=in_specs,
        out_specs=[tok(RWKV_WIDTH)] * 8 + [tok(2 * SB_WIDTH), tok(SB_WIDTH), tok(SB_WIDTH), tok(POOL_WIDTH)],
        out_shape=[shape(RWKV_WIDTH, F32)] * 8 + [shape(2 * SB_WIDTH, BF16), shape(SB_WIDTH, BF16),
                                                  shape(SB_WIDTH, BF16), shape(POOL_WIDTH, BF16)],
        compiler_params=_cparams("parallel"),
        name="front",
    )(*args)


def _pair_rows(x, lane_lo):
    return jnp.concatenate([jnp.where(lane_lo, x, 0.0), jnp.where(lane_lo, 0.0, x)], axis=0)


def _rwkv_chunk_kernel(chunks, r_ref, lw_ref, k_ref, v_ref, kk_ref, a_ref, tri_ref,
                       q1_o, q2_o, g_o, e_o):
    two = 2 * CHUNK
    ri = lax.broadcasted_iota(jnp.int32, (two, two), 0)
    ci = lax.broadcasted_iota(jnp.int32, (two, two), 1)
    same = (ri // CHUNK) == (ci // CHUNK)
    strict = same & ((ci % CHUNK) < (ri % CHUNK))
    incl = same & ((ci % CHUNK) <= (ri % CHUNK))
    eye = (ri == ci).astype(F32)
    lane_lo = lax.broadcasted_iota(jnp.int32, (CHUNK, PAIR), 1) < HEAD_DIM
    tri = tri_ref[...]

    units = []
    for c in range(chunks):
        rows = pl.ds(c * CHUNK, CHUNK)
        lw = lw_ref[rows, :]
        cl = _dot_exact_lhs(tri, lw)
        e_pos = jnp.exp(cl)
        e_prev = jnp.exp(cl - lw)
        e_neg = jnp.exp(-cl)
        kk = kk_ref[rows, :]
        at = -kk * e_prev
        bt = kk * a_ref[rows, :] * e_neg
        rt = r_ref[rows, :] * e_pos
        kt = k_ref[rows, :] * e_neg
        v = v_ref[rows, :]
        wc = e_pos[CHUNK - 1:CHUNK, :]
        for p in range(RWKV_PAIRS):
            ls = slice(p * PAIR, (p + 1) * PAIR)
            units.append(dict(
                c=c, p=p, rows=rows, ls=ls, wc=wc[:, ls],
                xa=_pair_rows(at[:, ls], lane_lo).astype(BF16), xr=_pair_rows(rt[:, ls], lane_lo),
                vv=_pair_rows(v[:, ls], lane_lo).astype(BF16), v=v[:, ls].astype(BF16),
                bt=bt[:, ls].astype(BF16), kt=kt[:, ls].astype(BF16)))

    for u in units:
        sc = _dot(jnp.concatenate([u["xa"], u["xr"].astype(BF16)], axis=0),
                  jnp.concatenate([u["bt"], u["bt"], u["kt"], u["kt"]], axis=0), NT)
        a_ab = jnp.where(strict, sc[:two, :two], 0.0)
        u["a_ak"] = jnp.where(strict, sc[:two, two:], 0.0).astype(BF16)
        u["m"] = jnp.concatenate([jnp.where(incl, sc[two:, :two], 0.0),
                                  jnp.where(incl, sc[two:, two:], 0.0)], axis=1).astype(BF16)
        u["t"] = eye + a_ab
        u["pw"] = a_ab.astype(BF16)

    for _ in range(CHUNK.bit_length() - 2):
        for u in units:
            u["pw"] = _dot(u["pw"], u["pw"]).astype(BF16)
        for u in units:
            u["t"] = u["t"] + _dot(u["t"].astype(BF16), u["pw"])

    for u in units:
        u["z"] = _dot(u["a_ak"], u["vv"]).astype(BF16)
    for u in units:
        u["pm"] = _dot(u["t"].astype(BF16), jnp.concatenate([u["xa"], u["z"]], axis=1))
    for u in units:
        vv = u["vv"]
        qm = _dot(u["m"], jnp.concatenate([u["pm"].astype(BF16),
                                           jnp.concatenate([jnp.zeros_like(vv), vv], axis=1)], axis=0))
        q1m = u["xr"] + qm[:, :PAIR]
        q2m = qm[:, PAIR:]
        q1_o[u["rows"], u["ls"]] = (q1m[:CHUNK] + q1m[CHUNK:]).astype(BF16)
        q2_o[u["rows"], u["ls"]] = q2m[:CHUNK] + q2m[CHUNK:]
    for u in units:
        pm = u["pm"]
        p12 = (pm[:CHUNK] + pm[CHUNK:]).astype(BF16)
        g_raw = _dot(p12[:, :PAIR], u["bt"], TN)
        e_raw = _dot(jnp.concatenate([p12[:, PAIR:], u["v"]], axis=0),
                     jnp.concatenate([u["bt"], u["kt"]], axis=0), TN)
        g_o[u["c"], u["p"]] = jnp.where(same, (eye + g_raw) * u["wc"], 0.0).astype(BF16)
        e_o[u["c"], u["p"]] = jnp.where(same, e_raw * u["wc"], 0.0)


def _rwkv_chunk(r, lw, k, v, kk, a, tri, chunks):
    n = r.shape[0]
    tc = chunks * CHUNK
    row = lambda i: (i, 0)
    tok = pl.BlockSpec((tc, RWKV_WIDTH), row)
    mat = pl.BlockSpec((chunks, RWKV_PAIRS, PAIR, PAIR), lambda i: (i, 0, 0, 0))
    tok_shape = lambda dtype: jax.ShapeDtypeStruct((n, RWKV_WIDTH), dtype)
    mat_shape = lambda dtype: jax.ShapeDtypeStruct((n // CHUNK, RWKV_PAIRS, PAIR, PAIR), dtype)
    return pl.pallas_call(
        functools.partial(_rwkv_chunk_kernel, chunks),
        grid=(n // tc,),
        in_specs=[tok] * 6 + [pl.BlockSpec((CHUNK, CHUNK), lambda i: (0, 0))],
        out_specs=[tok, tok, mat, mat],
        out_shape=[tok_shape(BF16), tok_shape(F32), mat_shape(BF16), mat_shape(F32)],
        compiler_params=_cparams("parallel"),
        name="rwkv_chunk",
    )(r, lw, k, v, kk, a, tri)


def _rwkv_scan_kernel(chunks, seqs, q1_ref, q2_ref, g_ref, e_ref, bonus_ref, gate_ref, hmean_ref,
                      lnw_ref, lnb_ref, o_ref, s_ref, y_ref):
    @pl.when(pl.program_id(1) == 0)
    def _():
        s_ref[...] = jnp.zeros_like(s_ref)

    for c in range(chunks):
        rows = pl.ds(c * CHUNK, CHUNK)
        for b in range(seqs):
            for p in range(RWKV_PAIRS):
                ls = slice(p * PAIR, (p + 1) * PAIR)
                s = s_ref[b, p]
                s_hi, s_lo = _split2(s)
                y_ref[b, rows, ls] = _dot(q1_ref[b, rows, ls], s_hi, NT) + q2_ref[b, rows, ls]
                g = g_ref[b, c, p]
                s_ref[b, p] = _dot(s_hi, g) + (_dot(s_lo, g) + e_ref[b, c, p])

    hmean = hmean_ref[...]
    for b in range(seqs):
        y = y_ref[b]
        mean = _dot_exact_rhs(y, hmean)
        yc = y - mean
        var = _head_reduce(yc * yc, hmean)
        yn = yc * lax.rsqrt(var + LN_X_EPS) * lnw_ref[...] + lnb_ref[...]
        o_ref[b] = ((yn + bonus_ref[b]) * gate_ref[b]).astype(o_ref.dtype)


def _rwkv_scan(q1, q2, g, e, bonus, gate, hmean, lnw, lnb, batch, seq_len, chunks):
    n = q1.shape[0]
    tc = chunks * CHUNK
    seqs = next(s for s in (4, 2, 1) if batch % s == 0)
    const = lambda b, i: (0, 0)
    tok = pl.BlockSpec((seqs, tc, RWKV_WIDTH), lambda b, i: (b, i, 0))
    mat = pl.BlockSpec((seqs, chunks, RWKV_PAIRS, PAIR, PAIR), lambda b, i: (b, i, 0, 0, 0))
    vec = pl.BlockSpec((1, RWKV_WIDTH), const)
    per_seq = lambda a: a.reshape((batch, a.shape[0] // batch) + a.shape[1:])
    out = pl.pallas_call(
        functools.partial(_rwkv_scan_kernel, chunks, seqs),
        grid=(batch // seqs, seq_len // tc),
        in_specs=[tok, tok, mat, mat, tok, tok,
                  pl.BlockSpec((RWKV_WIDTH, RWKV_WIDTH), const), vec, vec],
        out_specs=tok,
        out_shape=jax.ShapeDtypeStruct((batch, seq_len, RWKV_WIDTH), BF16),
        scratch_shapes=[pltpu.VMEM((seqs, RWKV_PAIRS, PAIR, PAIR), F32),
                        pltpu.VMEM((seqs, tc, RWKV_WIDTH), F32)],
        compiler_params=_cparams("parallel", "arbitrary"),
        name="rwkv_scan",
    )(per_seq(q1), per_seq(q2), per_seq(g), per_seq(e), per_seq(bonus), per_seq(gate), hmean, lnw, lnb)
    return out.reshape(n, RWKV_WIDTH)


def _sb_attn_kernel(tq, tk, q_ref, k_ref, v_ref, upper_ref, o_ref):
    qi = pl.program_id(2)
    sub = tq // tk
    neg_upper = upper_ref[...]
    first_row = [(sub - 1 - d) * tk for d in range(sub)]
    causal_of = [lax.broadcasted_iota(jnp.int32, (tq - r0, tk), 1) < lax.broadcasted_iota(jnp.int32, (tq - r0, tk), 0)
                 for r0 in first_row]

    def key_rows(j):
        return pl.ds(pl.multiple_of(j * tk, tk), tk)

    def consume(top, size, carry, diagonal_first):
        tiles = [(h, key_rows(top - d)) + ((first_row[d], causal_of[d]) if diagonal_first and d < sub else (0, None))
                 for d in range(size) for h in range(2)]
        carry = list(carry)
        log_betas, sps16, totals, logits = {}, {}, {}, {}
        for step in range(len(tiles) + 2):
            t = step
            if t < len(tiles):
                h, ks, r0, causal = tiles[t]
                z = _dot(q_ref[r0:, h * PAIR:(h + 1) * PAIR], k_ref[ks, :], NT)
                sp = jnp.maximum(z, 0.0) + jnp.log(1.0 + jnp.exp2(_neg_abs(z))) * LOG2_E
                if causal is not None:
                    sp = jnp.where(causal, sp, 0.0)
                sps16[t] = sp.astype(BF16)
                log_betas[t] = z - sp
                totals[t] = z[:, 0:1]
            t = step - 1
            if 0 <= t < len(tiles):
                logits[t] = log_betas.pop(t) + _dot(sps16.pop(t), neg_upper)
                totals[t] = totals[t] - logits[t][:, 0:1]
            t = step - 2
            if 0 <= t < len(tiles):
                h, ks, r0, causal = tiles[t]
                used, acc = carry[h]
                w = jnp.exp2(logits.pop(t) - used[r0:])
                if causal is not None:
                    w = jnp.where(causal, w, 0.0)
                used_new = used[r0:] + totals.pop(t)
                acc_new = acc[r0:] + _dot(w.astype(BF16), v_ref[ks, :])
                if r0:
                    used_new = jnp.concatenate([used[:r0], used_new], axis=0)
                    acc_new = jnp.concatenate([acc[:r0], acc_new], axis=0)
                carry[h] = (used_new, acc_new)
        return tuple(carry)

    zero = (jnp.zeros((tq, 1), F32), jnp.zeros((tq, PAIR), F32))
    below = qi * sub
    extra = below % KEY_BLOCKS
    top = below + sub - 1

    def first_group(e):
        if e + sub >= KEY_BLOCKS:
            return lambda c: consume(top, sub + e, c, True)
        return lambda c: lax.cond(extra == e, lambda c2: consume(top, sub + e, c2, True), first_group(e + sub), c)

    carry = first_group(0)((zero, zero))
    carry = lax.fori_loop(0, below // KEY_BLOCKS,
                          lambda i, c: consume(below - extra - 1 - KEY_BLOCKS * i, KEY_BLOCKS, c, False), carry)
    lane_lo = lax.broadcasted_iota(jnp.int32, (tq, PAIR), 1) < HEAD_DIM
    o_ref[...] = jnp.where(lane_lo, carry[0][1], carry[1][1]).astype(o_ref.dtype)


def _sb_attn(qm, kn, vb, upper, batch, seq_len, tq, tk):
    n = qm.shape[0]
    steps = seq_len // tq
    assert tq % tk == 0 and KEY_BLOCKS % (tq // tk) == 0
    q_spec = pl.BlockSpec((tq, 2 * PAIR), lambda b, p, i: (b * steps + i, p))
    k_spec = pl.BlockSpec((seq_len, PAIR), lambda b, p, i: (b, p))
    return pl.pallas_call(
        functools.partial(_sb_attn_kernel, tq, tk),
        grid=(batch, SB_PAIRS, steps),
        in_specs=[q_spec, k_spec, k_spec,
                  pl.BlockSpec((tk, tk), lambda b, p, i: (0, 0))],
        out_specs=pl.BlockSpec((tq, PAIR), lambda b, p, i: (b * steps + i, p)),
        out_shape=jax.ShapeDtypeStruct((n, SB_WIDTH), BF16),
        compiler_params=_cparams("parallel", "parallel", "arbitrary"),
        name="sb_attn",
    )(qm, kn, vb, upper)


def _mix_ffn_kernel(blocks_per_seq, tf, x_ref, xh_ref, yr_ref, yrh_ref, ys_ref, ysh_ref, yp_ref, yph_ref,
                    wo_ref, g_ref, wup_ref, cw_ref, cb_ref, wd_ref, o_ref, h_ref, act_ref):
    first = (pl.program_id(0) % blocks_per_seq) == 0
    g = g_ref[...]

    def mixed(x, yr, ys, yp):
        return x + _dot(jnp.concatenate([yr, ys, yp], axis=1), wo_ref[...])

    x = mixed(x_ref[...], yr_ref[...], ys_ref[...], yp_ref[...])
    tail = slice(BF16_ROWS - SUBLANES, BF16_ROWS)
    x_prev = mixed(xh_ref[...], yrh_ref[tail, :], ysh_ref[tail, :], yph_ref[tail, :])
    h_ref[0:SUBLANES, :] = jnp.where(first, 0.0, _rms_rows(x_prev, g)).astype(BF16)
    h_ref[SUBLANES:, :] = _rms_rows(x, g).astype(BF16)
    h = h_ref[...]

    def conv(col):
        cols = slice(col, col + tf)
        up = _dot(h, wup_ref[:, cols])
        cw = cw_ref[:, cols]
        c = cb_ref[:, cols] + up[SUBLANES:] * cw[2:3]
        c = c + pltpu.roll(up, 1, axis=0)[SUBLANES:] * cw[1:2]
        return c + pltpu.roll(up, 2, axis=0)[SUBLANES:] * cw[0:1]

    for j in range(D_FF // tf):
        gate = conv(j * tf)
        val = conv(D_FF + j * tf)
        act_ref[:, j * tf:(j + 1) * tf] = (gate * _sigmoid(gate) * val).astype(BF16)
    o_ref[...] = x + _dot(act_ref[...], wd_ref[...])


def _mix_ffn(x2, yr, ys, yp, w_out_bf16, g, w_up_bf16, conv_w, conv_b, w_down_bf16, layer, seq_len, tm, tf):
    n = x2.shape[0]
    row = lambda i: (i, 0)
    const = lambda i: (0, 0)
    halo = lambda rows: (lambda i: (jnp.maximum(i * (tm // rows) - 1, 0), 0))

    def with_halo(width, rows):
        return [pl.BlockSpec((tm, width), row), pl.BlockSpec((rows, width), halo(rows))]

    return pl.pallas_call(
        functools.partial(_mix_ffn_kernel, seq_len // tm, tf),
        grid=(n // tm,),
        in_specs=(with_halo(D_MODEL, SUBLANES) + with_halo(RWKV_WIDTH, BF16_ROWS) + with_halo(SB_WIDTH, BF16_ROWS)
                  + with_halo(POOL_WIDTH, BF16_ROWS)
                  + [_layer_weight(w_out_bf16, layer),
                     pl.BlockSpec((1, D_MODEL), const),
                     _layer_weight(w_up_bf16, layer),
                     pl.BlockSpec((3, 2 * D_FF), const),
                     pl.BlockSpec((1, 2 * D_FF), const),
                     _layer_weight(w_down_bf16, layer)]),
        out_specs=pl.BlockSpec((tm, D_MODEL), row),
        out_shape=jax.ShapeDtypeStruct((n, D_MODEL), F32),
        scratch_shapes=[pltpu.VMEM((tm + SUBLANES, D_MODEL), BF16),
                        pltpu.VMEM((tm, D_FF), BF16)],
        compiler_params=_cparams("parallel"),
        name="mix_ffn",
    )(x2, x2, yr, yr, ys, ys, yp, yp, w_out_bf16, g, w_up_bf16, conv_w, conv_b.reshape(1, 2 * D_FF), w_down_bf16)


def _head_indicator(width, value):
    idx = jnp.arange(width) // HEAD_DIM
    return jnp.where(idx[:, None] == idx[None, :], value, 0.0).astype(BF16)


def kernel(x, ln1_g, w_in, mu_shift, w0, w2, a0, a2, g2, k_k, k_a, r_k, lnx_w, lnx_b, v0, v1, v2, q_gain, k_gain, pool_w, pool_b, pool_scale, w_out, ln2_g, w_up, conv_w, conv_b, w_down):
    batch, seq_len, _ = x.shape
    depth = w_in.shape[0]
    n = batch * seq_len
    tm = min(TOKEN_BLOCK, seq_len)
    tq = min(QUERY_BLOCK, seq_len)
    tk = min(KEY_BLOCK, seq_len)
    scan_chunks = min(SCAN_CHUNKS, seq_len // CHUNK)
    assert seq_len % tm == 0 and seq_len % tq == 0 and seq_len % (scan_chunks * CHUNK) == 0
    assert tm % POOL_HALO == 0 and tm % BF16_ROWS == 0 and D_FF % FF_TILE == 0

    hsum = _head_indicator(RWKV_WIDTH, 1.0)
    hmean = _head_indicator(RWKV_WIDTH, 1.0 / HEAD_DIM)
    tri = jnp.tril(jnp.ones((CHUNK, CHUNK), F32)).astype(BF16)
    upper = (-jnp.tril(jnp.ones((tk, tk), F32), -1)).astype(BF16)
    row = lambda a: a.reshape(1, -1)
    w_in16, w_out16, w_up16, w_down16 = (w.astype(BF16) for w in (w_in, w_out, w_up, w_down))

    x2 = x.reshape(n, D_MODEL)
    v_first = None
    for l in range(depth):
        rwkv_params = (row(mu_shift[l]), row(w0[l]), w2[l].astype(BF16), row(a0[l]), a2[l].astype(BF16),
                       g2[l].astype(BF16), row(k_k[l]), row(k_a[l]), row(r_k[l]), hsum)
        vmix = None if l == 0 else (v_first, row(v0[l - 1]), v1[l - 1].astype(BF16), v2[l - 1].astype(BF16))
        sb_params = (row(jnp.tile(q_gain[l], PAIR // HEAD_DIM)), row(jnp.tile(k_gain[l], PAIR // HEAD_DIM)),
                     hmean[:PAIR, :PAIR])
        w_bd = jax.scipy.linalg.block_diag(*[pool_w[l, gi] for gi in range(len(POOL_WINDOWS))])
        pool_params = (w_bd.astype(BF16), row(pool_b[l]), row(pool_scale[l]))
        (r, lw, k, v, kk, a, gate, bonus, qm, kn, vb, y_pool) = _front(
            x2, row(ln1_g[l]), w_in16, l, rwkv_params, vmix, sb_params, pool_params, seq_len, tm)
        if l == 0:
            v_first = v
        q1, q2, g_mat, e_mat = _rwkv_chunk(r, lw, k, v, kk, a, tri, scan_chunks)
        y_rwkv = _rwkv_scan(q1, q2, g_mat, e_mat, bonus, gate, hmean, row(lnx_w[l]), row(lnx_b[l]),
                            batch, seq_len, scan_chunks)
        y_sb = _sb_attn(qm, kn, vb, upper, batch, seq_len, tq, tk)

        x2 = _mix_ffn(x2, y_rwkv, y_sb, y_pool, w_out16, row(ln2_g[l]), w_up16, conv_w[l], conv_b[l], w_down16,
                      l, seq_len, tm, FF_TILE)
    return x2.reshape(batch, seq_len, D_MODEL)
```

```python
import functools

import jax
import jax.numpy as jnp
from jax import lax
from jax.experimental import pallas as pl
from jax.experimental.pallas import tpu as pltpu

F32 = jnp.float32
BF16 = jnp.bfloat16

D_MODEL = 1024
HEAD_DIM = 64
RWKV_HEADS = 6
RWKV_WIDTH = RWKV_HEADS * HEAD_DIM
SB_HEADS = 6
SB_WIDTH = SB_HEADS * HEAD_DIM
POOL_WINDOWS = (2, 4, 8, 16)
POOL_WIDTH = D_MODEL - RWKV_WIDTH - SB_WIDTH
POOL_GROUP_DIM = POOL_WIDTH // len(POOL_WINDOWS)
DECAY_LORA = 64
AAA_LORA = 64
GATE_LORA = 128
MV_LORA = 32
RWKV_COLS = 3 * RWKV_WIDTH + DECAY_LORA + AAA_LORA + GATE_LORA
SB_COLS = 3 * SB_WIDTH
IN_COLS = RWKV_COLS + SB_COLS + POOL_WIDTH
D_FF = 2816
NORM_EPS = 1e-6
LN_X_EPS = 64e-5
L2_EPS = 1e-12
LOG2_E = 1.4426950408889634

SUBLANES = 8
BF16_ROWS = 2 * SUBLANES
PAIR = 2 * HEAD_DIM
RWKV_PAIRS = RWKV_WIDTH // PAIR
SB_PAIRS = SB_WIDTH // PAIR
CHUNK = 64
POOL_HALO = 16
VMEM_LIMIT = 48 * 1024 * 1024

TOKEN_BLOCK = 512
FF_TILE = 256
QUERY_BLOCK = 512
KEY_BLOCK = 256
KEY_BLOCKS = 12
SCAN_CHUNKS = 8

NN = (((1,), (0,)), ((), ()))
NT = (((1,), (1,)), ((), ()))
TN = (((0,), (0,)), ((), ()))


def _cparams(*sem):
    return pltpu.CompilerParams(dimension_semantics=sem, vmem_limit_bytes=VMEM_LIMIT)


def _dot(a, b, dims=NN):
    return lax.dot_general(a, b, dims, preferred_element_type=F32)


def _split2(x):
    hi = x.astype(BF16)
    lo = (x - hi.astype(F32)).astype(BF16)
    return hi, lo


def _neg_abs(x):
    return lax.bitcast_convert_type(lax.bitcast_convert_type(x, jnp.int32) | jnp.int32(-2 ** 31), F32)


def _split3(x):
    hi = x.astype(BF16)
    r1 = x - hi.astype(F32)
    mid = r1.astype(BF16)
    lo = (r1 - mid.astype(F32)).astype(BF16)
    return hi, mid, lo


def _head_reduce(a, indicator_bf16):
    return _dot(a.astype(BF16), indicator_bf16)


def _dot_exact_rhs(a, b_bf16, dims=NN):
    h, l = _split2(a)
    return _dot(h, b_bf16, dims) + _dot(l, b_bf16, dims)


def _dot_exact_lhs(a_bf16, b, dims=NN):
    h, m, l = _split3(b)
    return _dot(a_bf16, h, dims) + (_dot(a_bf16, m, dims) + _dot(a_bf16, l, dims))


def _sigmoid(x):
    return 1.0 / (1.0 + jnp.exp(-x))


def _softplus(x):
    return jnp.maximum(x, 0.0) + jnp.log(1.0 + jnp.exp(-jnp.abs(x)))


def _rms_rows(x, g):
    return x * lax.rsqrt(jnp.mean(x * x, axis=-1, keepdims=True) + NORM_EPS) * g


def _front_kernel(blocks_per_seq, has_vmix, *refs):
    refs = list(refs)
    x_ref, xh_ref, ln_ref, win_ref = refs[:4]
    mu_ref, w0_ref, w2_ref, a0_ref, a2_ref, g2_ref, kk_ref, ka_ref, rk_ref, hsum_ref = refs[4:14]
    refs = refs[14:]
    if has_vmix:
        vf_ref, v0_ref, v1_ref, v2_ref = refs[:4]
        refs = refs[4:]
    qg_ref, kg_ref, hmean_ref, pw_ref, pb_ref, ps_ref = refs[:6]
    r_o, lw_o, k_o, v_o, kk_o, a_o, g_o, bonus_o, q_o, kn_o, vb_o, pool_o = refs[6:]

    tm = x_ref.shape[0]
    blk = pl.program_id(0) % blocks_per_seq
    first = blk == 0
    ln = ln_ref[...]
    w_in = win_ref[...]
    proj = _dot(_rms_rows(x_ref[...], ln).astype(BF16), w_in)
    proj_rest = proj[:, RWKV_COLS:]
    proj_prev = jnp.where(first, 0.0, _dot(_rms_rows(xh_ref[...], ln).astype(BF16), w_in))

    p = proj[:, :RWKV_COLS]
    prev_last = proj_prev[POOL_HALO - 1:POOL_HALO, :RWKV_COLS]
    row = lax.broadcasted_iota(jnp.int32, (tm, 1), 0)
    shifted = jnp.where(row == 0, prev_last, pltpu.roll(p, 1, axis=0))
    p = p + (shifted - p) * mu_ref[...]

    c0, c1, c2 = RWKV_WIDTH, 2 * RWKV_WIDTH, 3 * RWKV_WIDTH
    r = p[:, 0:c0]
    k = p[:, c0:c1]
    v = p[:, c1:c2]
    xwa = p[:, c2:c2 + DECAY_LORA + AAA_LORA]
    xw = xwa[:, :DECAY_LORA]
    xa = xwa[:, DECAY_LORA:]
    xg = p[:, c2 + DECAY_LORA + AAA_LORA:]

    lora = lambda t, w_ref: _dot(t.astype(BF16), w_ref[...])
    w = -_softplus(-(w0_ref[...] + lora(jnp.tanh(xw), w2_ref))) - 0.5
    lw_o[...] = -jnp.exp(w)
    a = _sigmoid(a0_ref[...] + lora(xa, a2_ref))
    g_o[...] = lora(_sigmoid(xg), g2_ref)
    if has_vmix:
        mix = _sigmoid(v0_ref[...] + lora(lora(v, v1_ref), v2_ref))
        v = v + (vf_ref[...] - v) * mix
    hsum = hsum_ref[...]
    kk = k * kk_ref[...]
    kk = kk / jnp.maximum(jnp.sqrt(_head_reduce(kk * kk, hsum)), L2_EPS)
    k = k * (1.0 + (a - 1.0) * ka_ref[...])
    r_o[...] = r
    k_o[...] = k
    v_o[...] = v
    kk_o[...] = kk
    a_o[...] = a
    bonus_o[...] = _head_reduce(r * k * rk_ref[...], hsum) * v

    hmean = hmean_ref[...]
    lane_lo = lax.broadcasted_iota(jnp.int32, (tm, PAIR), 1) < HEAD_DIM
    for pair in range(SB_PAIRS):
        qc = pair * PAIR
        q = proj_rest[:, qc:qc + PAIR]
        kx = proj_rest[:, qc + SB_WIDTH:qc + SB_WIDTH + PAIR]
        qn = q * lax.rsqrt(_head_reduce(q * q, hmean) + NORM_EPS) * qg_ref[...] * (LOG2_E * HEAD_DIM ** -0.5)
        kn = kx * lax.rsqrt(_head_reduce(kx * kx, hmean) + NORM_EPS) * kg_ref[...]
        q_o[:, 2 * pair * PAIR:(2 * pair + 1) * PAIR] = jnp.where(lane_lo, qn, 0.0).astype(BF16)
        q_o[:, (2 * pair + 1) * PAIR:(2 * pair + 2) * PAIR] = jnp.where(lane_lo, 0.0, qn).astype(BF16)
        kn_o[:, pair * PAIR:(pair + 1) * PAIR] = kn.astype(BF16)
    vb_o[...] = proj_rest[:, 2 * SB_WIDTH:SB_COLS].astype(BF16)

    u = proj_rest[:, SB_COLS:]
    ext = jnp.concatenate([proj_prev[:, RWKV_COLS + SB_COLS:], u], axis=0)
    pos = (blk * tm + row).astype(F32)
    group = lax.broadcasted_iota(jnp.int32, (tm, POOL_WIDTH), 1) // POOL_GROUP_DIM
    acc = ext
    span = 1
    pooled = jnp.zeros((tm, POOL_WIDTH), F32)
    for gi, win in enumerate(POOL_WINDOWS):
        while span < win:
            acc = acc + pltpu.roll(acc, span, axis=0)
            span *= 2
        count = jnp.minimum(pos + 1.0, float(win))
        pooled = jnp.where(group == gi, acc[POOL_HALO:] / count, pooled)
    pooled = pooled - u
    pool_o[...] = ((_dot(pooled.astype(BF16), pw_ref[...]) + pb_ref[...]) * ps_ref[...]).astype(BF16)


def _layer_weight(stacked, layer):
    return pl.BlockSpec((None,) + stacked.shape[1:], lambda i: (layer, 0, 0), pipeline_mode=pl.Buffered(1))


def _front(x2, ln, w_in_bf16, layer, rwkv_params, vmix, sb_params, pool_params, seq_len, tm):
    n = x2.shape[0]
    has_vmix = vmix is not None
    row = lambda i: (i, 0)
    const = lambda i: (0, 0)
    halo_idx = lambda i: (jnp.maximum(i * (tm // POOL_HALO) - 1, 0), 0)
    full = lambda a: pl.BlockSpec(a.shape, const)
    in_specs = [pl.BlockSpec((tm, D_MODEL), row),
                pl.BlockSpec((POOL_HALO, D_MODEL), halo_idx),
                full(ln),
                _layer_weight(w_in_bf16, layer)]
    in_specs += [full(a) for a in rwkv_params]
    args = [x2, x2, ln, w_in_bf16] + list(rwkv_params)
    if has_vmix:
        in_specs += [pl.BlockSpec((tm, RWKV_WIDTH), row)] + [full(a) for a in vmix[1:]]
        args += list(vmix)
    in_specs += [full(a) for a in sb_params + pool_params]
    args += list(sb_params + pool_params)
    tok = lambda width: pl.BlockSpec((tm, width), row)
    shape = lambda width, dtype: jax.ShapeDtypeStruct((n, width), dtype)
    return pl.pallas_call(
        functools.partial(_front_kernel, seq_len // tm, has_vmix),
        grid=(n // tm,),
        in_specs=in_specs,
        out_specs=[tok(RWKV_WIDTH)] * 8 + [tok(2 * SB_WIDTH), tok(SB_WIDTH), tok(SB_WIDTH), tok(POOL_WIDTH)],
        out_shape=[shape(RWKV_WIDTH, F32)] * 8 + [shape(2 * SB_WIDTH, BF16), shape(SB_WIDTH, BF16),
                                                  shape(SB_WIDTH, BF16), shape(POOL_WIDTH, BF16)],
        compiler_params=_cparams("parallel"),
        name="front",
    )(*args)


def _pair_rows(x, lane_lo):
    return jnp.concatenate([jnp.where(lane_lo, x, 0.0), jnp.where(lane_lo, 0.0, x)], axis=0)


def _rwkv_chunk_kernel(chunks, r_ref, lw_ref, k_ref, v_ref, kk_ref, a_ref, tri_ref,
                       q1_o, q2_o, g_o, e_o):
    two = 2 * CHUNK
    ri = lax.broadcasted_iota(jnp.int32, (two, two), 0)
    ci = lax.broadcasted_iota(jnp.int32, (two, two), 1)
    same = (ri // CHUNK) == (ci // CHUNK)
    strict = same & ((ci % CHUNK) < (ri % CHUNK))
    incl = same & ((ci % CHUNK) <= (ri % CHUNK))
    eye = (ri == ci).astype(F32)
    lane_lo = lax.broadcasted_iota(jnp.int32, (CHUNK, PAIR), 1) < HEAD_DIM
    tri = tri_ref[...]

    units = []
    for c in range(chunks):
        rows = pl.ds(c * CHUNK, CHUNK)
        lw = lw_ref[rows, :]
        cl = _dot_exact_lhs(tri, lw)
        e_pos = jnp.exp(cl)
        e_prev = jnp.exp(cl - lw)
        e_neg = jnp.exp(-cl)
        kk = kk_ref[rows, :]
        at = -kk * e_prev
        bt = kk * a_ref[rows, :] * e_neg
        rt = r_ref[rows, :] * e_pos
        kt = k_ref[rows, :] * e_neg
        v = v_ref[rows, :]
        wc = e_pos[CHUNK - 1:CHUNK, :]
        for p in range(RWKV_PAIRS):
            ls = slice(p * PAIR, (p + 1) * PAIR)
            units.append(dict(
                c=c, p=p, rows=rows, ls=ls, wc=wc[:, ls],
                xa=_pair_rows(at[:, ls], lane_lo).astype(BF16), xr=_pair_rows(rt[:, ls], lane_lo),
                vv=_pair_rows(v[:, ls], lane_lo).astype(BF16), v=v[:, ls].astype(BF16),
                bt=bt[:, ls].astype(BF16), kt=kt[:, ls].astype(BF16)))

    for u in units:
        sc = _dot(jnp.concatenate([u["xa"], u["xr"].astype(BF16)], axis=0),
                  jnp.concatenate([u["bt"], u["bt"], u["kt"], u["kt"]], axis=0), NT)
        a_ab = jnp.where(strict, sc[:two, :two], 0.0)
        u["a_ak"] = jnp.where(strict, sc[:two, two:], 0.0).astype(BF16)
        u["m"] = jnp.concatenate([jnp.where(incl, sc[two:, :two], 0.0),
                                  jnp.where(incl, sc[two:, two:], 0.0)], axis=1).astype(BF16)
        u["t"] = eye + a_ab
        u["pw"] = a_ab.astype(BF16)

    for _ in range(CHUNK.bit_length() - 2):
        for u in units:
            u["pw"] = _dot(u["pw"], u["pw"]).astype(BF16)
        for u in units:
            u["t"] = u["t"] + _dot(u["t"].astype(BF16), u["pw"])

    for u in units:
        u["z"] = _dot(u["a_ak"], u["vv"]).astype(BF16)
    for u in units:
        u["pm"] = _dot(u["t"].astype(BF16), jnp.concatenate([u["xa"], u["z"]], axis=1))
    for u in units:
        vv = u["vv"]
        qm = _dot(u["m"], jnp.concatenate([u["pm"].astype(BF16),
                                           jnp.concatenate([jnp.zeros_like(vv), vv], axis=1)], axis=0))
        q1m = u["xr"] + qm[:, :PAIR]
        q2m = qm[:, PAIR:]
        q1_o[u["rows"], u["ls"]] = (q1m[:CHUNK] + q1m[CHUNK:]).astype(BF16)
        q2_o[u["rows"], u["ls"]] = q2m[:CHUNK] + q2m[CHUNK:]
    for u in units:
        pm = u["pm"]
        p12 = (pm[:CHUNK] + pm[CHUNK:]).astype(BF16)
        g_raw = _dot(p12[:, :PAIR], u["bt"], TN)
        e_raw = _dot(jnp.concatenate([p12[:, PAIR:], u["v"]], axis=0),
                     jnp.concatenate([u["bt"], u["kt"]], axis=0), TN)
        g_o[u["c"], u["p"]] = jnp.where(same, (eye + g_raw) * u["wc"], 0.0).astype(BF16)
        e_o[u["c"], u["p"]] = jnp.where(same, e_raw * u["wc"], 0.0)


def _rwkv_chunk(r, lw, k, v, kk, a, tri, chunks):
    n = r.shape[0]
    tc = chunks * CHUNK
    row = lambda i: (i, 0)
    tok = pl.BlockSpec((tc, RWKV_WIDTH), row)
    mat = pl.BlockSpec((chunks, RWKV_PAIRS, PAIR, PAIR), lambda i: (i, 0, 0, 0))
    tok_shape = lambda dtype: jax.ShapeDtypeStruct((n, RWKV_WIDTH), dtype)
    mat_shape = lambda dtype: jax.ShapeDtypeStruct((n // CHUNK, RWKV_PAIRS, PAIR, PAIR), dtype)
    return pl.pallas_call(
        functools.partial(_rwkv_chunk_kernel, chunks),
        grid=(n // tc,),
        in_specs=[tok] * 6 + [pl.BlockSpec((CHUNK, CHUNK), lambda i: (0, 0))],
        out_specs=[tok, tok, mat, mat],
        out_shape=[tok_shape(BF16), tok_shape(F32), mat_shape(BF16), mat_shape(F32)],
        compiler_params=_cparams("parallel"),
        name="rwkv_chunk",
    )(r, lw, k, v, kk, a, tri)


def _rwkv_scan_kernel(chunks, seqs, q1_ref, q2_ref, g_ref, e_ref, bonus_ref, gate_ref, hmean_ref,
                      lnw_ref, lnb_ref, o_ref, s_ref, y_ref):
    @pl.when(pl.program_id(1) == 0)
    def _():
        s_ref[...] = jnp.zeros_like(s_ref)

    for c in range(chunks):
        rows = pl.ds(c * CHUNK, CHUNK)
        for b in range(seqs):
            for p in range(RWKV_PAIRS):
                ls = slice(p * PAIR, (p + 1) * PAIR)
                s = s_ref[b, p]
                s_hi, s_lo = _split2(s)
                y_ref[b, rows, ls] = _dot(q1_ref[b, rows, ls], s_hi, NT) + q2_ref[b, rows, ls]
                g = g_ref[b, c, p]
                s_ref[b, p] = _dot(s_hi, g) + (_dot(s_lo, g) + e_ref[b, c, p])

    hmean = hmean_ref[...]
    for b in range(seqs):
        y = y_ref[b]
        mean = _dot_exact_rhs(y, hmean)
        yc = y - mean
        var = _head_reduce(yc * yc, hmean)
        yn = yc * lax.rsqrt(var + LN_X_EPS) * lnw_ref[...] + lnb_ref[...]
        o_ref[b] = ((yn + bonus_ref[b]) * gate_ref[b]).astype(o_ref.dtype)


def _rwkv_scan(q1, q2, g, e, bonus, gate, hmean, lnw, lnb, batch, seq_len, chunks):
    n = q1.shape[0]
    tc = chunks * CHUNK
    seqs = next(s for s in (4, 2, 1) if batch % s == 0)
    const = lambda b, i: (0, 0)
    tok = pl.BlockSpec((seqs, tc, RWKV_WIDTH), lambda b, i: (b, i, 0))
    mat = pl.BlockSpec((seqs, chunks, RWKV_PAIRS, PAIR, PAIR), lambda b, i: (b, i, 0, 0, 0))
    vec = pl.BlockSpec((1, RWKV_WIDTH), const)
    per_seq = lambda a: a.reshape((batch, a.shape[0] // batch) + a.shape[1:])
    out = pl.pallas_call(
        functools.partial(_rwkv_scan_kernel, chunks, seqs),
        grid=(batch // seqs, seq_len // tc),
        in_specs=[tok, tok, mat, mat, tok, tok,
                  pl.BlockSpec((RWKV_WIDTH, RWKV_WIDTH), const), vec, vec],
        out_specs=tok,
        out_shape=jax.ShapeDtypeStruct((batch, seq_len, RWKV_WIDTH), BF16),
        scratch_shapes=[pltpu.VMEM((seqs, RWKV_PAIRS, PAIR, PAIR), F32),
                        pltpu.VMEM((seqs, tc, RWKV_WIDTH), F32)],
        compiler_params=_cparams("parallel", "arbitrary"),
        name="rwkv_scan",
    )(per_seq(q1), per_seq(q2), per_seq(g), per_seq(e), per_seq(bonus), per_seq(gate), hmean, lnw, lnb)
    return out.reshape(n, RWKV_WIDTH)


def _sb_attn_kernel(tq, tk, q_ref, k_ref, v_ref, upper_ref, o_ref):
    qi = pl.program_id(2)
    sub = tq // tk
    neg_upper = upper_ref[...]
    first_row = [(sub - 1 - d) * tk for d in range(sub)]
    causal_of = [lax.broadcasted_iota(jnp.int32, (tq - r0, tk), 1) < lax.broadcasted_iota(jnp.int32, (tq - r0, tk), 0)
                 for r0 in first_row]

    def key_rows(j):
        return pl.ds(pl.multiple_of(j * tk, tk), tk)

    def consume(top, size, carry, diagonal_first):
        tiles = [(h, key_rows(top - d)) + ((first_row[d], causal_of[d]) if diagonal_first and d < sub else (0, None))
                 for d in range(size) for h in range(2)]
        carry = list(carry)
        log_betas, sps16, totals, logits = {}, {}, {}, {}
        for step in range(len(tiles) + 2):
            t = step
            if t < len(tiles):
                h, ks, r0, causal = tiles[t]
                z = _dot(q_ref[r0:, h * PAIR:(h + 1) * PAIR], k_ref[ks, :], NT)
                sp = jnp.maximum(z, 0.0) + jnp.log(1.0 + jnp.exp2(_neg_abs(z))) * LOG2_E
                if causal is not None:
                    sp = jnp.where(causal, sp, 0.0)
                sps16[t] = sp.astype(BF16)
                log_betas[t] = z - sp
                totals[t] = z[:, 0:1]
            t = step - 1
            if 0 <= t < len(tiles):
                logits[t] = log_betas.pop(t) + _dot(sps16.pop(t), neg_upper)
                totals[t] = totals[t] - logits[t][:, 0:1]
            t = step - 2
            if 0 <= t < len(tiles):
                h, ks, r0, causal = tiles[t]
                used, acc = carry[h]
                w = jnp.exp2(logits.pop(t) - used[r0:])
                if causal is not None:
                    w = jnp.where(causal, w, 0.0)
                used_new = used[r0:] + totals.pop(t)
                acc_new = acc[r0:] + _dot(w.astype(BF16), v_ref[ks, :])
                if r0:
                    used_new = jnp.concatenate([used[:r0], used_new], axis=0)
                    acc_new = jnp.concatenate([acc[:r0], acc_new], axis=0)
                carry[h] = (used_new, acc_new)
        return tuple(carry)

    zero = (jnp.zeros((tq, 1), F32), jnp.zeros((tq, PAIR), F32))
    below = qi * sub
    extra = below % KEY_BLOCKS
    top = below + sub - 1

    def first_group(e):
        if e + sub >= KEY_BLOCKS:
            return lambda c: consume(top, sub + e, c, True)
        return lambda c: lax.cond(extra == e, lambda c2: consume(top, sub + e, c2, True), first_group(e + sub), c)

    carry = first_group(0)((zero, zero))
    carry = lax.fori_loop(0, below // KEY_BLOCKS,
                          lambda i, c: consume(below - extra - 1 - KEY_BLOCKS * i, KEY_BLOCKS, c, False), carry)
    lane_lo = lax.broadcasted_iota(jnp.int32, (tq, PAIR), 1) < HEAD_DIM
    o_ref[...] = jnp.where(lane_lo, carry[0][1], carry[1][1]).astype(o_ref.dtype)


def _sb_attn(qm, kn, vb, upper, batch, seq_len, tq, tk):
    n = qm.shape[0]
    steps = seq_len // tq
    assert tq % tk == 0 and KEY_BLOCKS % (tq // tk) == 0
    q_spec = pl.BlockSpec((tq, 2 * PAIR), lambda b, p, i: (b * steps + i, p))
    k_spec = pl.BlockSpec((seq_len, PAIR), lambda b, p, i: (b, p))
    return pl.pallas_call(
        functools.partial(_sb_attn_kernel, tq, tk),
        grid=(batch, SB_PAIRS, steps),
        in_specs=[q_spec, k_spec, k_spec,
                  pl.BlockSpec((tk, tk), lambda b, p, i: (0, 0))],
        out_specs=pl.BlockSpec((tq, PAIR), lambda b, p, i: (b * steps + i, p)),
        out_shape=jax.ShapeDtypeStruct((n, SB_WIDTH), BF16),
        compiler_params=_cparams("parallel", "parallel", "arbitrary"),
        name="sb_attn",
    )(qm, kn, vb, upper)


def _mix_ffn_kernel(blocks_per_seq, tf, x_ref, xh_ref, yr_ref, yrh_ref, ys_ref, ysh_ref, yp_ref, yph_ref,
                    wo_ref, g_ref, wup_ref, cw_ref, cb_ref, wd_ref, o_ref, h_ref, act_ref):
    first = (pl.program_id(0) % blocks_per_seq) == 0
    g = g_ref[...]

    def mixed(x, yr, ys, yp):
        return x + _dot(jnp.concatenate([yr, ys, yp], axis=1), wo_ref[...])

    x = mixed(x_ref[...], yr_ref[...], ys_ref[...], yp_ref[...])
    tail = slice(BF16_ROWS - SUBLANES, BF16_ROWS)
    x_prev = mixed(xh_ref[...], yrh_ref[tail, :], ysh_ref[tail, :], yph_ref[tail, :])
    h_ref[0:SUBLANES, :] = jnp.where(first, 0.0, _rms_rows(x_prev, g)).astype(BF16)
    h_ref[SUBLANES:, :] = _rms_rows(x, g).astype(BF16)
    h = h_ref[...]

    def conv(col):
        cols = slice(col, col + tf)
        up = _dot(h, wup_ref[:, cols])
        cw = cw_ref[:, cols]
        c = cb_ref[:, cols] + up[SUBLANES:] * cw[2:3]
        c = c + pltpu.roll(up, 1, axis=0)[SUBLANES:] * cw[1:2]
        return c + pltpu.roll(up, 2, axis=0)[SUBLANES:] * cw[0:1]

    for j in range(D_FF // tf):
        gate = conv(j * tf)
        val = conv(D_FF + j * tf)
        act_ref[:, j * tf:(j + 1) * tf] = (gate * _sigmoid(gate) * val).astype(BF16)
    o_ref[...] = x + _dot(act_ref[...], wd_ref[...])


def _mix_ffn(x2, yr, ys, yp, w_out_bf16, g, w_up_bf16, conv_w, conv_b, w_down_bf16, layer, seq_len, tm, tf):
    n = x2.shape[0]
    row = lambda i: (i, 0)
    const = lambda i: (0, 0)
    halo = lambda rows: (lambda i: (jnp.maximum(i * (tm // rows) - 1, 0), 0))

    def with_halo(width, rows):
        return [pl.BlockSpec((tm, width), row), pl.BlockSpec((rows, width), halo(rows))]

    return pl.pallas_call(
        functools.partial(_mix_ffn_kernel, seq_len // tm, tf),
        grid=(n // tm,),
        in_specs=(with_halo(D_MODEL, SUBLANES) + with_halo(RWKV_WIDTH, BF16_ROWS) + with_halo(SB_WIDTH, BF16_ROWS)
                  + with_halo(POOL_WIDTH, BF16_ROWS)
                  + [_layer_weight(w_out_bf16, layer),
                     pl.BlockSpec((1, D_MODEL), const),
                     _layer_weight(w_up_bf16, layer),
                     pl.BlockSpec((3, 2 * D_FF), const),
                     pl.BlockSpec((1, 2 * D_FF), const),
                     _layer_weight(w_down_bf16, layer)]),
        out_specs=pl.BlockSpec((tm, D_MODEL), row),
        out_shape=jax.ShapeDtypeStruct((n, D_MODEL), F32),
        scratch_shapes=[pltpu.VMEM((tm + SUBLANES, D_MODEL), BF16),
                        pltpu.VMEM((tm, D_FF), BF16)],
        compiler_params=_cparams("parallel"),
        name="mix_ffn",
    )(x2, x2, yr, yr, ys, ys, yp, yp, w_out_bf16, g, w_up_bf16, conv_w, conv_b.reshape(1, 2 * D_FF), w_down_bf16)


def _head_indicator(width, value):
    idx = jnp.arange(width) // HEAD_DIM
    return jnp.where(idx[:, None] == idx[None, :], value, 0.0).astype(BF16)


def kernel(x, ln1_g, w_in, mu_shift, w0, w2, a0, a2, g2, k_k, k_a, r_k, lnx_w, lnx_b, v0, v1, v2, q_gain, k_gain, pool_w, pool_b, pool_scale, w_out, ln2_g, w_up, conv_w, conv_b, w_down):
    batch, seq_len, _ = x.shape
    depth = w_in.shape[0]
    n = batch * seq_len
    tm = min(TOKEN_BLOCK, seq_len)
    tq = min(QUERY_BLOCK, seq_len)
    tk = min(KEY_BLOCK, seq_len)
    scan_chunks = min(SCAN_CHUNKS, seq_len // CHUNK)
    assert seq_len % tm == 0 and seq_len % tq == 0 and seq_len % (scan_chunks * CHUNK) == 0
    assert tm % POOL_HALO == 0 and tm % BF16_ROWS == 0 and D_FF % FF_TILE == 0

    hsum = _head_indicator(RWKV_WIDTH, 1.0)
    hmean = _head_indicator(RWKV_WIDTH, 1.0 / HEAD_DIM)
    tri = jnp.tril(jnp.ones((CHUNK, CHUNK), F32)).astype(BF16)
    upper = (-jnp.tril(jnp.ones((tk, tk), F32), -1)).astype(BF16)
    row = lambda a: a.reshape(1, -1)
    w_in16, w_out16, w_up16, w_down16 = (w.astype(BF16) for w in (w_in, w_out, w_up, w_down))

    x2 = x.reshape(n, D_MODEL)
    v_first = None
    for l in range(depth):
        rwkv_params = (row(mu_shift[l]), row(w0[l]), w2[l].astype(BF16), row(a0[l]), a2[l].astype(BF16),
                       g2[l].astype(BF16), row(k_k[l]), row(k_a[l]), row(r_k[l]), hsum)
        vmix = None if l == 0 else (v_first, row(v0[l - 1]), v1[l - 1].astype(BF16), v2[l - 1].astype(BF16))
        sb_params = (row(jnp.tile(q_gain[l], PAIR // HEAD_DIM)), row(jnp.tile(k_gain[l], PAIR // HEAD_DIM)),
                     hmean[:PAIR, :PAIR])
        w_bd = jax.scipy.linalg.block_diag(*[pool_w[l, gi] for gi in range(len(POOL_WINDOWS))])
        pool_params = (w_bd.astype(BF16), row(pool_b[l]), row(pool_scale[l]))
        (r, lw, k, v, kk, a, gate, bonus, qm, kn, vb, y_pool) = _front(
            x2, row(ln1_g[l]), w_in16, l, rwkv_params, vmix, sb_params, pool_params, seq_len, tm)
        if l == 0:
            v_first = v
        q1, q2, g_mat, e_mat = _rwkv_chunk(r, lw, k, v, kk, a, tri, scan_chunks)
        y_rwkv = _rwkv_scan(q1, q2, g_mat, e_mat, bonus, gate, hmean, row(lnx_w[l]), row(lnx_b[l]),
                            batch, seq_len, scan_chunks)
        y_sb = _sb_attn(qm, kn, vb, upper, batch, seq_len, tq, tk)

        x2 = _mix_ffn(x2, y_rwkv, y_sb, y_pool, w_out16, row(ln2_g[l]), w_up16, conv_w[l], conv_b[l], w_down16,
                      l, seq_len, tm, FF_TILE)
    return x2.reshape(batch, seq_len, D_MODEL)
```
